```python
import jax, jax.numpy as jnp
from jax import lax
import numpy as np

D_MODEL = 2048
BATCH = 1
SEQ = 8192
DEPTH = 1

CHUNK = 64
D_MIX = D_MODEL
GMLP_HEADS = 8
GMLP_HEAD_DIM = 128
GMLP_WIDTH = GMLP_HEADS * GMLP_HEAD_DIM
GMLP_BLOCK = 128
MLA_HEADS = 8
MLA_NOPE = 128
MLA_ROPE = 64
MLA_QK = MLA_NOPE + MLA_ROPE
MLA_V = 128
MLA_WIDTH = MLA_HEADS * MLA_V
MLA_Q_RANK = 512
MLA_KV_RANK = 256
ROPE_THETA = 10000.0
Q_BLOCK = 128
PEER_HEADS = 8
PEER_N_KEYS = 128
PEER_N_EXPERTS = PEER_N_KEYS * PEER_N_KEYS
PEER_D_KEY = 256
PEER_D_HALF = PEER_D_KEY // 2
PEER_TOPK = 16
PEER_TOKEN_BLOCK = 128
RMS_EPS = 1e-6
IN_COLS = 2 * GMLP_WIDTH + MLA_Q_RANK + MLA_KV_RANK + MLA_ROPE
SPLIT_POINTS = (2 * GMLP_WIDTH, 2 * GMLP_WIDTH + MLA_Q_RANK, 2 * GMLP_WIDTH + MLA_Q_RANK + MLA_KV_RANK)

kernel_name = "hybrid_gmlp_mla_peer_block"


def rmsnorm(x, g):
    xf = x.astype(jnp.float32)
    xf = xf * lax.rsqrt(jnp.mean(xf * xf, axis=-1, keepdims=True) + RMS_EPS)
    return xf.astype(x.dtype) * g


def rope_tables(seq_len, dtype):
    pos = jnp.arange(seq_len, dtype=jnp.float32)
    inv_freq = ROPE_THETA ** (-jnp.arange(0, MLA_ROPE, 2, dtype=jnp.float32) / MLA_ROPE)
    ang = pos[:, None] * inv_freq[None, :]
    return jnp.cos(ang).astype(dtype), jnp.sin(ang).astype(dtype)


def apply_rope(x, cos, sin):
    c = cos[None, :, None, :]
    s = sin[None, :, None, :]
    x1, x2 = jnp.split(x, 2, axis=-1)
    return jnp.concatenate([x1 * c - x2 * s, x1 * s + x2 * c], axis=-1)


def gmlp_mixer(z, vnorm_g, ws, bs):
    bsz, seq, _ = z.shape
    z = jax.nn.gelu(z)
    u, v = jnp.split(z, 2, axis=-1)
    v = rmsnorm(v, vnorm_g)
    nblk = seq // GMLP_BLOCK
    v = v.reshape(bsz, nblk, GMLP_BLOCK, GMLP_HEADS, GMLP_HEAD_DIM)
    mask = jnp.tril(jnp.ones((GMLP_BLOCK, GMLP_BLOCK), dtype=bool))
    w = jnp.where(mask[None], ws, jnp.zeros_like(ws))
    sv = jnp.einsum('hij,bnjhc->bnihc', w, v) + bs.T[None, None, :, :, None]
    return u * sv.reshape(bsz, seq, GMLP_WIDTH)


def mla_mixer(c_q, c_kv, k_rope, qnorm_g, w_uq, kvnorm_g, w_ukv, qk_q_g, qk_k_g, cos, sin):
    bsz, seq, _ = c_q.shape
    q = (rmsnorm(c_q, qnorm_g) @ w_uq).reshape(bsz, seq, MLA_HEADS, MLA_QK)
    kv = (rmsnorm(c_kv, kvnorm_g) @ w_ukv).reshape(bsz, seq, MLA_HEADS, MLA_NOPE + MLA_V)
    k_nope, v = kv[..., :MLA_NOPE], kv[..., MLA_NOPE:]
    k_r = jnp.broadcast_to(k_rope[:, :, None, :], (bsz, seq, MLA_HEADS, MLA_ROPE))
    k = jnp.concatenate([k_nope, k_r], axis=-1)
    q = rmsnorm(q, qk_q_g)
    k = rmsnorm(k, qk_k_g)
    q = jnp.concatenate([q[..., :MLA_NOPE], apply_rope(q[..., MLA_NOPE:], cos, sin)], axis=-1)
    k = jnp.concatenate([k[..., :MLA_NOPE], apply_rope(k[..., MLA_NOPE:], cos, sin)], axis=-1)
    q = q * (MLA_QK ** -0.5)
    key_chunk = jnp.arange(seq) // CHUNK

    def attend_block(i):
        start = i * Q_BLOCK
        qb = lax.dynamic_slice_in_dim(q, start, Q_BLOCK, axis=1)
        s = jnp.einsum('bqhd,bkhd->bhqk', qb, k).astype(jnp.float32)
        q_chunk = (start + jnp.arange(Q_BLOCK)) // CHUNK
        allowed = key_chunk[None, :] <= q_chunk[:, None]
        s = jnp.where(allowed[None, None], s, -jnp.inf)
        p = jax.nn.softmax(s, axis=-1).astype(v.dtype)
        return jnp.einsum('bhqk,bkhd->bqhd', p, v)

    out = lax.map(attend_block, jnp.arange(seq // Q_BLOCK))
    return out.transpose(1, 0, 2, 3, 4).reshape(bsz, seq, MLA_WIDTH)


def peer_ffn(x, w_q, subkeys, u_tab, v_tab):
    bsz, seq, d = x.shape
    xb = x.reshape(bsz * seq // PEER_TOKEN_BLOCK, PEER_TOKEN_BLOCK, d)

    def retrieve(xblk):
        t = xblk.shape[0]
        q = (xblk @ w_q).reshape(t, PEER_HEADS, 2, PEER_D_HALF)
        s = jnp.einsum('thpd,hpnd->thpn', q, subkeys).astype(jnp.float32)
        sv, si = lax.top_k(s, PEER_TOPK)
        cand = (sv[:, :, 0, :, None] + sv[:, :, 1, None, :]).reshape(t, PEER_HEADS, PEER_TOPK * PEER_TOPK)
        cand_id = (si[:, :, 0, :, None] * PEER_N_KEYS + si[:, :, 1, None, :]).reshape(t, PEER_HEADS, PEER_TOPK * PEER_TOPK)
        top_s, top_pos = lax.top_k(cand, PEER_TOPK)
        experts = jnp.take_along_axis(cand_id, top_pos, axis=-1)
        g = jax.nn.softmax(top_s, axis=-1).astype(xblk.dtype)
        u = u_tab[experts]
        a = jax.nn.gelu(jnp.einsum('thkd,td->thk', u, xblk))
        vv = v_tab[experts]
        return jnp.einsum('thk,thkd->td', g * a, vv)

    return lax.map(retrieve, xb).reshape(bsz, seq, d)


def setup_inputs(seed: int = 0) -> dict:
    key = jax.random.key(seed)
    ks = jax.random.split(key, 18)
    f32 = jnp.float32
    nrm = lambda k, shape, scale: jax.random.normal(k, shape, f32) * scale
    gain = lambda k, n: 1.0 + 0.05 * jax.random.normal(k, (DEPTH, n), f32)
    return {
        "x": nrm(ks[0], (BATCH, SEQ, D_MODEL), 1.0),
        "norm_mix_g": gain(ks[1], D_MODEL),
        "w_in": nrm(ks[2], (DEPTH, D_MODEL, IN_COLS), D_MODEL ** -0.5),
        "gmlp_vnorm_g": gain(ks[3], GMLP_WIDTH),
        "gmlp_ws": nrm(ks[4], (DEPTH, GMLP_HEADS, GMLP_BLOCK, GMLP_BLOCK), GMLP_BLOCK ** -0.5),
        "gmlp_bs": 1.0 + nrm(ks[5], (DEPTH, GMLP_HEADS, GMLP_BLOCK), 0.1),
        "mla_qnorm_g": gain(ks[6], MLA_Q_RANK),
        "mla_w_uq": nrm(ks[7], (DEPTH, MLA_Q_RANK, MLA_HEADS * MLA_QK), MLA_Q_RANK ** -0.5),
        "mla_kvnorm_g": gain(ks[8], MLA_KV_RANK),
        "mla_w_ukv": nrm(ks[9], (DEPTH, MLA_KV_RANK, MLA_HEADS * (MLA_NOPE + MLA_V)), MLA_KV_RANK ** -0.5),
        "mla_qk_q_g": gain(ks[10], MLA_QK),
        "mla_qk_k_g": gain(ks[11], MLA_QK),
        "w_out": nrm(ks[12], (DEPTH, D_MIX, D_MODEL), D_MIX ** -0.5),
        "norm_ffn_g": gain(ks[13], D_MODEL),
        "peer_w_q": nrm(ks[14], (DEPTH, D_MODEL, PEER_HEADS * PEER_D_KEY), D_MODEL ** -0.5),
        "peer_subkeys": nrm(ks[15], (DEPTH, PEER_HEADS, 2, PEER_N_KEYS, PEER_D_HALF), PEER_D_HALF ** -0.5),
        "peer_u": nrm(ks[16], (DEPTH, PEER_N_EXPERTS, D_MODEL), D_MODEL ** -0.5),
        "peer_v": nrm(ks[17], (DEPTH, PEER_N_EXPERTS, D_MODEL), PEER_HEADS ** -0.5),
    }


def reference(x, norm_mix_g, w_in, gmlp_vnorm_g, gmlp_ws, gmlp_bs, mla_qnorm_g, mla_w_uq,
              mla_kvnorm_g, mla_w_ukv, mla_qk_q_g, mla_qk_k_g, w_out, norm_ffn_g,
              peer_w_q, peer_subkeys, peer_u, peer_v):
    cos, sin = rope_tables(x.shape[1], x.dtype)
    for l in range(DEPTH):
        h = rmsnorm(x, norm_mix_g[l])
        proj = h @ w_in[l]
        z_a, c_q, c_kv, k_rope = jnp.split(proj, SPLIT_POINTS, axis=-1)
        out_a = gmlp_mixer(z_a, gmlp_vnorm_g[l], gmlp_ws[l], gmlp_bs[l])
        out_b = mla_mixer(c_q, c_kv, k_rope, mla_qnorm_g[l], mla_w_uq[l], mla_kvnorm_g[l],
                          mla_w_ukv[l], mla_qk_q_g[l], mla_qk_k_g[l], cos, sin)
        x = x + jnp.concatenate([out_a, out_b], axis=-1) @ w_out[l]
        x = x + peer_ffn(rmsnorm(x, norm_ffn_g[l]), peer_w_q[l], peer_subkeys[l], peer_u[l], peer_v[l])
    return x
```

```python
import functools

import jax
import jax.numpy as jnp
import numpy as np
from jax import lax
from jax.experimental import pallas as pl
from jax.experimental.pallas import tpu as pltpu

F32 = jnp.float32
BF16 = jnp.bfloat16

D_MODEL = 2048
CHUNK = 64
GMLP_HEADS = 8
GMLP_WIDTH = 1024
GMLP_BLOCK = 128
MLA_HEADS = 8
MLA_NOPE = 128
MLA_ROPE = 64
MLA_QK = MLA_NOPE + MLA_ROPE
MLA_Q_RANK = 512
MLA_KV_RANK = 256
ROPE_THETA = 10000.0
PEER_HEADS = 8
PEER_N_KEYS = 128
PEER_TOPK = 16
RMS_EPS = 1e-6
LANES = 128
QK_PAD = 2 * LANES

TM_FRONT = 256
TQ_ATTN = 512
TK_ATTN = 512
TM_MID = 256
TT_PEER = 512
EC_PEER = 1024
TM_FINAL = 256
VMEM_LIMIT = 56 * 1024 * 1024


def _rms_scale(v, width):
    return lax.rsqrt(jnp.sum(v * v, axis=-1, keepdims=True) * (1.0 / width) + RMS_EPS)


def _front_kernel(x_ref, gmix_ref, win_ref, gv_ref, ws_ref, bsb_ref, gq_ref, wuq_ref, gkv_ref, wukv_ref,
                  gqn_ref, gqr_ref, gqrr_ref, gkn_ref, gkr_ref, gkrr_ref, cos_ref, sin_ref,
                  oa_ref, q_ref, k_ref, v_ref):
    tm = x_ref.shape[0]
    x = x_ref[...]
    h = x * _rms_scale(x, D_MODEL) * gmix_ref[...]
    proj = jnp.dot(h.astype(BF16), win_ref[...], preferred_element_type=F32)

    zu = jax.nn.gelu(proj[:, :GMLP_WIDTH])
    zv = jax.nn.gelu(proj[:, GMLP_WIDTH:2 * GMLP_WIDTH])
    vn = (zv * _rms_scale(zv, GMLP_WIDTH) * gv_ref[...]).astype(BF16)
    row = lax.broadcasted_iota(jnp.int32, (GMLP_BLOCK, GMLP_BLOCK), 0)
    col = lax.broadcasted_iota(jnp.int32, (GMLP_BLOCK, GMLP_BLOCK), 1)
    tri = col <= row
    for hh in range(GMLP_HEADS):
        w = jnp.where(tri, ws_ref[hh], 0.0).astype(BF16)
        cs = slice(hh * LANES, (hh + 1) * LANES)
        for r in range(tm // GMLP_BLOCK):
            rs = slice(r * GMLP_BLOCK, (r + 1) * GMLP_BLOCK)
            sv = jnp.dot(w, vn[rs, cs], preferred_element_type=F32) + bsb_ref[:, cs]
            oa_ref[rs, cs] = (zu[rs, cs] * sv).astype(BF16)

    c0 = 2 * GMLP_WIDTH
    cq = proj[:, c0:c0 + MLA_Q_RANK]
    ckv = proj[:, c0 + MLA_Q_RANK:c0 + MLA_Q_RANK + MLA_KV_RANK]
    c1 = c0 + MLA_Q_RANK + MLA_KV_RANK
    kr = proj[:, c1:c1 + LANES]
    krr = proj[:, c1 + LANES:c1 + 2 * LANES]
    cqn = (cq * _rms_scale(cq, MLA_Q_RANK) * gq_ref[...]).astype(BF16)
    qall = jnp.dot(cqn, wuq_ref[...], preferred_element_type=F32)
    ckvn = (ckv * _rms_scale(ckv, MLA_KV_RANK) * gkv_ref[...]).astype(BF16)
    kvall = jnp.dot(ckvn, wukv_ref[...], preferred_element_type=F32)

    cos_f = cos_ref[...]
    sin_f = sin_ref[...]
    lane = lax.broadcasted_iota(jnp.int32, (tm, LANES), 1)
    lo = lane < MLA_ROPE
    k_rot = kr * gkr_ref[...] * cos_f + krr * gkrr_ref[...] * sin_f
    ssq_kr = jnp.sum(jnp.where(lo, kr * kr, 0.0), axis=-1, keepdims=True)
    q_scale = MLA_QK ** -0.5
    nh = MLA_HEADS * MLA_NOPE
    for g in range(MLA_HEADS // 2):
        qr = qall[:, nh + g * LANES:nh + (g + 1) * LANES]
        qrr = qall[:, nh + 512 + g * LANES:nh + 512 + (g + 1) * LANES]
        q_rot = qr * gqr_ref[...] * cos_f + qrr * gqrr_ref[...] * sin_f
        qr2 = qr * qr
        ssq_half = (jnp.sum(jnp.where(lo, qr2, 0.0), axis=-1, keepdims=True),
                    jnp.sum(jnp.where(lo, 0.0, qr2), axis=-1, keepdims=True))
        for p in range(2):
            hh = 2 * g + p
            hs = slice(hh * LANES, (hh + 1) * LANES)
            qn = qall[:, hs]
            rinv = lax.rsqrt((jnp.sum(qn * qn, axis=-1, keepdims=True) + ssq_half[p]) * (1.0 / MLA_QK)
                             + RMS_EPS) * q_scale
            mine = lo if p == 0 else jnp.logical_not(lo)
            q_ref[hh, :, :LANES] = (qn * gqn_ref[...] * rinv).astype(BF16)
            q_ref[hh, :, LANES:] = (jnp.where(mine, q_rot, 0.0) * rinv).astype(BF16)
            kn = kvall[:, hs]
            rinvk = lax.rsqrt((jnp.sum(kn * kn, axis=-1, keepdims=True) + ssq_kr) * (1.0 / MLA_QK) + RMS_EPS)
            k_ref[hh, :, :LANES] = (kn * gkn_ref[...] * rinvk).astype(BF16)
            k_ref[hh, :, LANES:] = (k_rot * rinvk).astype(BF16)
            v_ref[hh] = kvall[:, nh + hh * LANES:nh + (hh + 1) * LANES].astype(BF16)


def _attn_kernel(q_ref, k_ref, v_ref, o_ref):
    tq = q_ref.shape[0]
    qi = pl.program_id(1)
    q = q_ref[...]

    def tile(kt, carry, masked):
        m, l, acc = carry
        start = pl.multiple_of(kt * TK_ATTN, TK_ATTN)
        kk = k_ref[pl.ds(start, TK_ATTN), :]
        vv = v_ref[pl.ds(start, TK_ATTN), :]
        s = lax.dot_general(q, kk, (((1,), (1,)), ((), ())), preferred_element_type=F32)
        if masked:
            r_chunk = lax.broadcasted_iota(jnp.int32, (tq, TK_ATTN), 0) // CHUNK
            c_chunk = lax.broadcasted_iota(jnp.int32, (tq, TK_ATTN), 1) // CHUNK
            s = jnp.where(c_chunk <= r_chunk, s, -jnp.inf)
        m_new = jnp.maximum(m, jnp.max(s, axis=-1, keepdims=True))
        alpha = jnp.exp(m - m_new)
        p = jnp.exp(s - m_new)
        l = alpha * l + jnp.sum(p, axis=-1, keepdims=True)
        acc = alpha * acc + jnp.dot(p.astype(BF16), vv, preferred_element_type=F32)
        return m_new, l, acc

    init = (jnp.full((tq, 1), -jnp.inf, F32), jnp.zeros((tq, 1), F32), jnp.zeros((tq, LANES), F32))
    carry = lax.fori_loop(0, qi, lambda kt, c: tile(kt, c, False), init)
    m, l, acc = tile(qi, carry, True)
    o_ref[...] = (acc / l).astype(BF16)


def _topk_ranks(s):
    n = s.shape[0]
    idx = lax.broadcasted_iota(jnp.int32, s.shape, 0)
    rank = jnp.full(s.shape, float(PEER_TOPK), F32)
    vals = []
    for kk in range(PEER_TOPK):
        m = jnp.max(s, axis=0, keepdims=True)
        first = jnp.min(jnp.where(s == m, idx, n), axis=0, keepdims=True)
        sel = idx == first
        rank = jnp.where(sel, float(kk), rank)
        s = jnp.where(sel, -jnp.inf, s)
        vals.append(m)
    return jnp.concatenate(vals, axis=0), rank


def _pair_counts(v1, v2):
    t = v1.shape[1]
    row8 = lax.broadcasted_iota(jnp.int32, (8, t), 0)
    cands = [v1[0:1] + v2]
    for a in range(1, 8):
        cands.append(jnp.where(row8 < PEER_TOPK // (a + 1), v1[a:a + 1] + v2[0:8], -jnp.inf))
    cands.append(v1[8:16] + v2[0:1])
    work = list(cands)
    cum = jnp.zeros((1, t), F32)
    tau = jnp.zeros((1, t), F32)
    above = jnp.zeros((1, t), F32)
    for _ in range(PEER_TOPK):
        m = functools.reduce(jnp.maximum, [jnp.max(c, axis=0, keepdims=True) for c in work])
        hit = [c == m for c in work]
        cnt = functools.reduce(jnp.add, [jnp.sum(jnp.where(hh, 1.0, 0.0), axis=0, keepdims=True) for hh in hit])
        active = cum < PEER_TOPK
        tau = jnp.where(active, m, tau)
        above = jnp.where(active, cum, above)
        cum = cum + cnt
        work = [jnp.where(hh, -jnp.inf, c) for hh, c in zip(hit, work)]
    need = PEER_TOPK - above
    n_rows = []
    for a in range(PEER_TOPK):
        c = cands[a] if a < 8 else cands[8][a - 8:a - 7]
        gt = jnp.sum(jnp.where(c > tau, 1.0, 0.0), axis=0, keepdims=True)
        eq = jnp.sum(jnp.where(c == tau, 1.0, 0.0), axis=0, keepdims=True)
        take = jnp.minimum(eq, jnp.maximum(need, 0.0))
        need = need - eq
        n_rows.append(gt + take)
    return jnp.concatenate(n_rows, axis=0)


def _mid_kernel(x_ref, oa_ref, ob_ref, woa_ref, wob_ref, gffn_ref, wqt_ref, sk_ref,
                x1_ref, xnt_ref, n1_ref, a_ref, r2_ref, b_ref):
    tm = x_ref.shape[0]
    x1 = (x_ref[...] + jnp.dot(oa_ref[...], woa_ref[...], preferred_element_type=F32)
          + jnp.dot(ob_ref[...], wob_ref[...], preferred_element_type=F32))
    x1_ref[...] = x1
    xn = x1 * _rms_scale(x1, D_MODEL) * gffn_ref[...]
    xnt = xn.T.astype(BF16)
    xnt_ref[...] = xnt
    qt = jnp.dot(wqt_ref[...], xnt, preferred_element_type=F32).astype(BF16)
    rowk = lax.broadcasted_iota(jnp.int32, (PEER_TOPK, tm), 0).astype(F32)
    for hh in range(PEER_HEADS):
        s, vals, rank = [], [], []
        for p in range(2):
            hp = 2 * hh + p
            sp = jnp.dot(sk_ref[hp], qt[hp * LANES:(hp + 1) * LANES, :], preferred_element_type=F32)
            vp, rp = _topk_ranks(sp)
            s.append(sp), vals.append(vp), rank.append(rp)
        n = _pair_counts(vals[0], vals[1])
        e1 = jnp.exp(vals[0] - vals[0][0:1])
        e2 = jnp.exp(vals[1] - vals[1][0:1])
        z = jnp.zeros((1, tm), F32)
        for a in range(PEER_TOPK):
            z = z + e1[a:a + 1] * jnp.sum(jnp.where(rowk < n[a:a + 1], e2, 0.0), axis=0, keepdims=True)
        n1 = jnp.zeros((PEER_N_KEYS, tm), F32)
        for a in range(PEER_TOPK):
            n1 = jnp.where(rank[0] == float(a), n[a:a + 1], n1)
        n1_ref[hh] = n1
        a_ref[hh] = jnp.exp(s[0] - vals[0][0:1]) / z
        r2_ref[hh] = rank[1].astype(BF16)
        b_ref[hh] = jnp.exp(s[1] - vals[1][0:1]).astype(BF16)


def _peer_kernel(n1_ref, a_ref, r2_ref, b_ref, xnt_ref, u_ref, vt_ref, out_ref):
    e = pl.program_id(1)
    tt = xnt_ref.shape[1]
    at = jnp.dot(u_ref[...], xnt_ref[...], preferred_element_type=F32)
    parts = []
    for ii in range(EC_PEER // PEER_N_KEYS):
        w = jnp.zeros((PEER_N_KEYS, tt), BF16)
        for hh in range(PEER_HEADS):
            n1row = jnp.broadcast_to(n1_ref[hh, ii:ii + 1, :], (PEER_N_KEYS, tt)).astype(BF16)
            arow = jnp.broadcast_to(a_ref[hh, ii:ii + 1, :], (PEER_N_KEYS, tt)).astype(BF16)
            w = w + jnp.where(r2_ref[hh] < n1row, b_ref[hh], jnp.zeros((), BF16)) * arow
        act = jax.nn.gelu(at[ii * PEER_N_KEYS:(ii + 1) * PEER_N_KEYS, :]).astype(BF16)
        parts.append(act * w)
    wa = jnp.concatenate(parts, axis=0)
    contrib = jnp.dot(vt_ref[...], wa, preferred_element_type=F32)

    @pl.when(e == 0)
    def _():
        out_ref[...] = contrib

    @pl.when(e != 0)
    def _():
        out_ref[...] += contrib


def _final_kernel(x1_ref, outt_ref, y_ref):
    y_ref[...] = x1_ref[...] + outt_ref[...].T


def _const_spec(shape):
    zeros = (0,) * len(shape)
    return pl.BlockSpec(shape, lambda *_: zeros)


def _params(semantics):
    return pltpu.CompilerParams(dimension_semantics=semantics, vmem_limit_bytes=VMEM_LIMIT)


def kernel(x, norm_mix_g, w_in, gmlp_vnorm_g, gmlp_ws, gmlp_bs, mla_qnorm_g, mla_w_uq, mla_kvnorm_g, mla_w_ukv,
           mla_qk_q_g, mla_qk_k_g, w_out, norm_ffn_g, peer_w_q, peer_subkeys, peer_u, peer_v):
    bsz, seq, d = x.shape
    assert bsz == 1 and d == D_MODEL and seq % TQ_ATTN == 0 and seq % TT_PEER == 0
    x2 = x[0]
    n_exp = peer_u.shape[1]

    rot = (np.arange(MLA_ROPE) + MLA_ROPE // 2) % MLA_ROPE
    rot2 = np.concatenate([rot, rot + MLA_ROPE])
    c_rope = 2 * GMLP_WIDTH + MLA_Q_RANK + MLA_KV_RANK
    w_rope = w_in[0][:, c_rope:]
    w_in_ext = jnp.concatenate([w_in[0][:, :c_rope], w_rope, w_rope, w_rope[:, rot], w_rope[:, rot]],
                               axis=1).astype(BF16)
    wq3 = mla_w_uq[0].reshape(MLA_Q_RANK, MLA_HEADS, MLA_QK)
    wq_rope = wq3[:, :, MLA_NOPE:]
    w_uq_ext = jnp.concatenate([wq3[:, :, :MLA_NOPE].reshape(MLA_Q_RANK, -1),
                                wq_rope.reshape(MLA_Q_RANK, -1),
                                wq_rope[:, :, rot].reshape(MLA_Q_RANK, -1)], axis=1).astype(BF16)
    wkv3 = mla_w_ukv[0].reshape(MLA_KV_RANK, MLA_HEADS, MLA_NOPE + LANES)
    w_ukv_ext = jnp.concatenate([wkv3[:, :, :MLA_NOPE].reshape(MLA_KV_RANK, -1),
                                 wkv3[:, :, MLA_NOPE:].reshape(MLA_KV_RANK, -1)], axis=1).astype(BF16)
    row = lambda v: v.reshape(1, -1)
    gq, gk = mla_qk_q_g[0], mla_qk_k_g[0]
    dup = lambda v: row(jnp.concatenate([v, v]))
    bsb = jnp.repeat(gmlp_bs[0].T, LANES, axis=1)
    pos = jnp.arange(seq, dtype=F32)
    inv_freq = ROPE_THETA ** (-jnp.arange(0, MLA_ROPE, 2, dtype=F32) / MLA_ROPE)
    ang = pos[:, None] * inv_freq[None, :]
    cos, sin = jnp.cos(ang), jnp.sin(ang)
    cos_f = jnp.concatenate([cos, cos, cos, cos], axis=1)
    sin_f = jnp.concatenate([-sin, sin, -sin, sin], axis=1)

    n_front = seq // TM_FRONT
    rows = lambda w: pl.BlockSpec((TM_FRONT, w), lambda i: (i, 0))
    heads = lambda w: pl.BlockSpec((MLA_HEADS, TM_FRONT, w), lambda i: (0, i, 0))
    oa, q, k, v = pl.pallas_call(
        _front_kernel,
        grid=(n_front,),
        in_specs=[rows(D_MODEL), _const_spec((1, D_MODEL)), _const_spec(w_in_ext.shape),
                  _const_spec((1, GMLP_WIDTH)), _const_spec(gmlp_ws[0].shape), _const_spec(bsb.shape),
                  _const_spec((1, MLA_Q_RANK)), _const_spec(w_uq_ext.shape),
                  _const_spec((1, MLA_KV_RANK)), _const_spec(w_ukv_ext.shape),
                  _const_spec((1, LANES)), _const_spec((1, LANES)), _const_spec((1, LANES)),
                  _const_spec((1, LANES)), _const_spec((1, LANES)), _const_spec((1, LANES)),
                  rows(LANES), rows(LANES)],
        out_specs=[rows(GMLP_WIDTH), heads(QK_PAD), heads(QK_PAD), heads(LANES)],
        out_shape=[jax.ShapeDtypeStruct((seq, GMLP_WIDTH), BF16),
                   jax.ShapeDtypeStruct((MLA_HEADS, seq, QK_PAD), BF16),
                   jax.ShapeDtypeStruct((MLA_HEADS, seq, QK_PAD), BF16),
                   jax.ShapeDtypeStruct((MLA_HEADS, seq, LANES), BF16)],
        compiler_params=_params(("parallel",)),
        name="front",
    )(x2, row(norm_mix_g[0]), w_in_ext, row(gmlp_vnorm_g[0]), gmlp_ws[0], bsb,
      row(mla_qnorm_g[0]), w_uq_ext, row(mla_kvnorm_g[0]), w_ukv_ext,
      row(gq[:MLA_NOPE]), dup(gq[MLA_NOPE:]), dup(gq[MLA_NOPE:][rot]),
      row(gk[:MLA_NOPE]), dup(gk[MLA_NOPE:]), dup(gk[MLA_NOPE:][rot]), cos_f, sin_f)

    ob = pl.pallas_call(
        _attn_kernel,
        grid=(MLA_HEADS, seq // TQ_ATTN),
        in_specs=[pl.BlockSpec((None, TQ_ATTN, QK_PAD), lambda h, i: (h, i, 0)),
                  pl.BlockSpec((None, seq, QK_PAD), lambda h, i: (h, 0, 0)),
                  pl.BlockSpec((None, seq, LANES), lambda h, i: (h, 0, 0))],
        out_specs=pl.BlockSpec((TQ_ATTN, LANES), lambda h, i: (i, h)),
        out_shape=jax.ShapeDtypeStruct((seq, MLA_HEADS * LANES), BF16),
        compiler_params=_params(("parallel", "parallel")),
        name="attn",
    )(q, k, v)

    wo = w_out[0].astype(BF16)
    wqt = peer_w_q[0].T.astype(BF16)
    sk = peer_subkeys[0].reshape(PEER_HEADS * 2, PEER_N_KEYS, -1).astype(BF16)
    n_mid = seq // TM_MID
    mrows = lambda w: pl.BlockSpec((TM_MID, w), lambda i: (i, 0))
    tok = pl.BlockSpec((PEER_HEADS, PEER_N_KEYS, TM_MID), lambda i: (0, 0, i))
    tok_shape = lambda dt: jax.ShapeDtypeStruct((PEER_HEADS, PEER_N_KEYS, seq), dt)
    x1, xnt, n1, a1, r2, b2 = pl.pallas_call(
        _mid_kernel,
        grid=(n_mid,),
        in_specs=[mrows(D_MODEL), mrows(GMLP_WIDTH), mrows(MLA_HEADS * LANES),
                  _const_spec((GMLP_WIDTH, D_MODEL)), _const_spec((MLA_HEADS * LANES, D_MODEL)),
                  _const_spec((1, D_MODEL)), _const_spec(wqt.shape), _const_spec(sk.shape)],
        out_specs=[mrows(D_MODEL), pl.BlockSpec((D_MODEL, TM_MID), lambda i: (0, i)), tok, tok, tok, tok],
        out_shape=[jax.ShapeDtypeStruct((seq, D_MODEL), F32), jax.ShapeDtypeStruct((D_MODEL, seq), BF16),
                   tok_shape(F32), tok_shape(F32), tok_shape(BF16), tok_shape(BF16)],
        compiler_params=_params(("parallel",)),
        name="mid",
    )(x2, oa, ob, wo[:GMLP_WIDTH], wo[GMLP_WIDTH:], row(norm_ffn_g[0]), wqt, sk)

    u_bf = peer_u[0].astype(BF16)
    vt_bf = peer_v[0].T.astype(BF16)
    keys_per_step = EC_PEER // PEER_N_KEYS
    first_half = pl.BlockSpec((PEER_HEADS, keys_per_step, TT_PEER), lambda t, e: (0, e, t))
    second_half = pl.BlockSpec((PEER_HEADS, PEER_N_KEYS, TT_PEER), lambda t, e: (0, 0, t))
    outt = pl.pallas_call(
        _peer_kernel,
        grid=(seq // TT_PEER, n_exp // EC_PEER),
        in_specs=[first_half, first_half, second_half, second_half,
                  pl.BlockSpec((D_MODEL, TT_PEER), lambda t, e: (0, t)),
                  pl.BlockSpec((EC_PEER, D_MODEL), lambda t, e: (e, 0)),
                  pl.BlockSpec((D_MODEL, EC_PEER), lambda t, e: (0, e))],
        out_specs=pl.BlockSpec((D_MODEL, TT_PEER), lambda t, e: (0, t)),
        out_shape=jax.ShapeDtypeStruct((D_MODEL, seq), F32),
        compiler_params=_params(("parallel", "arbitrary")),
        name="peer",
    )(n1, a1, r2, b2, xnt, u_bf, vt_bf)

    y = pl.pallas_call(
        _final_kernel,
        grid=(seq // TM_FINAL,),
        in_specs=[pl.BlockSpec((TM_FINAL, D_MODEL), lambda i: (i, 0)),
                  pl.BlockSpec((D_MODEL, TM_FINAL), lambda i: (0, i))],
        out_specs=pl.BlockSpec((TM_FINAL, D_MODEL), lambda i: (i, 0)),
        out_shape=jax.ShapeDtypeStruct((seq, D_MODEL), F32),
        compiler_params=_params(("parallel",)),
        name="final",
    )(x1, outt)
    return y[None]
```

```python
import functools

import jax
import jax.numpy as jnp
import numpy as np
from jax import lax
from jax.experimental import pallas as pl
from jax.experimental.pallas import tpu as pltpu

F32 = jnp.float32
BF16 = jnp.bfloat16

D_MODEL = 2048
CHUNK = 64
GMLP_HEADS = 8
GMLP_WIDTH = 1024
GMLP_BLOCK = 128
MLA_HEADS = 8
MLA_NOPE = 128
MLA_ROPE = 64
MLA_QK = MLA_NOPE + MLA_ROPE
MLA_Q_RANK = 512
MLA_KV_RANK = 256
ROPE_THETA = 10000.0
PEER_HEADS = 8
PEER_N_KEYS = 128
PEER_TOPK = 16
RMS_EPS = 1e-6
LANES = 128
QK_PAD = 2 * LANES

TM_FRONT = 256
TQ_ATTN = 512
TK_ATTN = 512
ATTN_HEADS = 2
TM_MID = 256
TT_PEER = 512
EC_PEER = 1024
PEER_SUB = 256
VMEM_LIMIT = 56 * 1024 * 1024


def _rms_scale(v, width):
    return lax.rsqrt(jnp.sum(v * v, axis=-1, keepdims=True) * (1.0 / width) + RMS_EPS)


def _front_kernel(x_ref, gmix_ref, win_ref, gv_ref, ws_ref, bsb_ref, gq_ref, wuq_ref, gkv_ref, wukv_ref,
                  gqn_ref, gqr_ref, gqrr_ref, gkn_ref, gkr_ref, gkrr_ref, cos_ref, sin_ref,
                  oa_ref, q_ref, k_ref, v_ref):
    tm = x_ref.shape[0]
    x = x_ref[...]
    h = x * _rms_scale(x, D_MODEL) * gmix_ref[...]
    proj = jnp.dot(h.astype(BF16), win_ref[...], preferred_element_type=F32)

    zu = jax.nn.gelu(proj[:, :GMLP_WIDTH])
    zv = jax.nn.gelu(proj[:, GMLP_WIDTH:2 * GMLP_WIDTH])
    vn = (zv * _rms_scale(zv, GMLP_WIDTH) * gv_ref[...]).astype(BF16)
    row = lax.broadcasted_iota(jnp.int32, (GMLP_BLOCK, GMLP_BLOCK), 0)
    col = lax.broadcasted_iota(jnp.int32, (GMLP_BLOCK, GMLP_BLOCK), 1)
    tri = col <= row
    for hh in range(GMLP_HEADS):
        w = jnp.where(tri, ws_ref[hh], 0.0).astype(BF16)
        cs = slice(hh * LANES, (hh + 1) * LANES)
        for r in range(tm // GMLP_BLOCK):
            rs = slice(r * GMLP_BLOCK, (r + 1) * GMLP_BLOCK)
            sv = jnp.dot(w, vn[rs, cs], preferred_element_type=F32) + bsb_ref[:, cs]
            oa_ref[rs, cs] = (zu[rs, cs] * sv).astype(BF16)

    c0 = 2 * GMLP_WIDTH
    cq = proj[:, c0:c0 + MLA_Q_RANK]
    ckv = proj[:, c0 + MLA_Q_RANK:c0 + MLA_Q_RANK + MLA_KV_RANK]
    c1 = c0 + MLA_Q_RANK + MLA_KV_RANK
    kr = proj[:, c1:c1 + LANES]
    krr = proj[:, c1 + LANES:c1 + 2 * LANES]
    cqn = (cq * _rms_scale(cq, MLA_Q_RANK) * gq_ref[...]).astype(BF16)
    qall = jnp.dot(cqn, wuq_ref[...], preferred_element_type=F32)
    ckvn = (ckv * _rms_scale(ckv, MLA_KV_RANK) * gkv_ref[...]).astype(BF16)
    kvall = jnp.dot(ckvn, wukv_ref[...], preferred_element_type=F32)

    cos_f = cos_ref[...]
    sin_f = sin_ref[...]
    lane = lax.broadcasted_iota(jnp.int32, (tm, LANES), 1)
    lo = lane < MLA_ROPE
    k_rot = kr * gkr_ref[...] * cos_f + krr * gkrr_ref[...] * sin_f
    ssq_kr = jnp.sum(jnp.where(lo, kr * kr, 0.0), axis=-1, keepdims=True)
    q_scale = MLA_QK ** -0.5
    nh = MLA_HEADS * MLA_NOPE
    for g in range(MLA_HEADS // 2):
        qr = qall[:, nh + g * LANES:nh + (g + 1) * LANES]
        qrr = qall[:, nh + 512 + g * LANES:nh + 512 + (g + 1) * LANES]
        q_rot = qr * gqr_ref[...] * cos_f + qrr * gqrr_ref[...] * sin_f
        qr2 = qr * qr
        ssq_half = (jnp.sum(jnp.where(lo, qr2, 0.0), axis=-1, keepdims=True),
                    jnp.sum(jnp.where(lo, 0.0, qr2), axis=-1, keepdims=True))
        for p in range(2):
            hh = 2 * g + p
            hs = slice(hh * LANES, (hh + 1) * LANES)
            qn = qall[:, hs]
            rinv = lax.rsqrt((jnp.sum(qn * qn, axis=-1, keepdims=True) + ssq_half[p]) * (1.0 / MLA_QK)
                             + RMS_EPS) * q_scale
            mine = lo if p == 0 else jnp.logical_not(lo)
            q_ref[hh, :, :LANES] = (qn * gqn_ref[...] * rinv).astype(BF16)
            q_ref[hh, :, LANES:] = (jnp.where(mine, q_rot, 0.0) * rinv).astype(BF16)
            kn = kvall[:, hs]
            rinvk = lax.rsqrt((jnp.sum(kn * kn, axis=-1, keepdims=True) + ssq_kr) * (1.0 / MLA_QK) + RMS_EPS)
            k_ref[hh, :, :LANES] = (kn * gkn_ref[...] * rinvk).astype(BF16)
            k_ref[hh, :, LANES:] = (k_rot * rinvk).astype(BF16)
            v_ref[hh] = kvall[:, nh + hh * LANES:nh + (hh + 1) * LANES].astype(BF16)


def _attn_kernel(q_ref, k_ref, v_ref, u_ref, pv_ref, o_ref, ub_ref, vt_ref):
    tq = q_ref.shape[1]
    qi = pl.program_id(1)
    ub_ref[...] = u_ref[...].astype(BF16)
    vt_ref[...] = pv_ref[...].T.astype(BF16)
    qs = [q_ref[j] for j in range(ATTN_HEADS)]

    def tile(kt, carry, masked):
        start = pl.multiple_of(kt * TK_ATTN, TK_ATTN)
        out = []
        for j in range(ATTN_HEADS):
            m, l, acc = carry[j]
            kk = k_ref[j, pl.ds(start, TK_ATTN), :]
            vv = v_ref[j, pl.ds(start, TK_ATTN), :]
            s = lax.dot_general(qs[j], kk, (((1,), (1,)), ((), ())), preferred_element_type=F32)
            if masked:
                r_chunk = lax.broadcasted_iota(jnp.int32, (tq, TK_ATTN), 0) // CHUNK
                c_chunk = lax.broadcasted_iota(jnp.int32, (tq, TK_ATTN), 1) // CHUNK
                s = jnp.where(c_chunk <= r_chunk, s, -jnp.inf)
            m_new = jnp.maximum(m, jnp.max(s, axis=-1, keepdims=True))
            alpha = jnp.exp(m - m_new)
            p = jnp.exp(s - m_new)
            l = alpha * l + jnp.sum(p, axis=-1, keepdims=True)
            acc = alpha * acc + jnp.dot(p.astype(BF16), vv, preferred_element_type=F32)
            out.append((m_new, l, acc))
        return tuple(out)

    init = tuple((jnp.full((tq, 1), -jnp.inf, F32), jnp.zeros((tq, 1), F32), jnp.zeros((tq, LANES), F32))
                 for _ in range(ATTN_HEADS))
    carry = lax.fori_loop(0, qi, lambda kt, c: tile(kt, c, False), init)
    carry = tile(qi, carry, True)
    for j in range(ATTN_HEADS):
        _, l, acc = carry[j]
        o_ref[:, j * LANES:(j + 1) * LANES] = (acc / l).astype(BF16)


def _topk_ranks(s):
    n = s.shape[0]
    idx = lax.broadcasted_iota(jnp.int32, s.shape, 0)
    rank = jnp.full(s.shape, float(PEER_TOPK), F32)
    vals = []
    for kk in range(PEER_TOPK):
        m = jnp.max(s, axis=0, keepdims=True)
        first = jnp.min(jnp.where(s == m, idx, n), axis=0, keepdims=True)
        sel = idx == first
        rank = jnp.where(sel, float(kk), rank)
        s = jnp.where(sel, -jnp.inf, s)
        vals.append(m)
    return jnp.concatenate(vals, axis=0), rank


def _pair_counts(v1, v2):
    t = v1.shape[1]
    row8 = lax.broadcasted_iota(jnp.int32, (8, t), 0)
    cands = [v1[0:1] + v2[0:8], v1[0:1] + v2[8:16]]
    for a in range(1, 8):
        cands.append(jnp.where(row8 < PEER_TOPK // (a + 1), v1[a:a + 1] + v2[0:8], -jnp.inf))
    cands.append(v1[8:16] + v2[0:1])
    work = list(cands)
    cum = jnp.zeros((1, t), F32)
    tau = jnp.zeros((1, t), F32)
    above = jnp.zeros((1, t), F32)
    for _ in range(PEER_TOPK):
        m = jnp.max(functools.reduce(jnp.maximum, work), axis=0, keepdims=True)
        hit = [c == m for c in work]
        cnt = jnp.sum(functools.reduce(jnp.add, [jnp.where(hh, 1.0, 0.0) for hh in hit]), axis=0, keepdims=True)
        active = cum < PEER_TOPK
        tau = jnp.where(active, m, tau)
        above = jnp.where(active, cum, above)
        cum = cum + cnt
        work = [jnp.where(hh, -jnp.inf, c) for hh, c in zip(hit, work)]
    need = PEER_TOPK - above
    rows = [[cands[0], cands[1]]] + [[cands[a + 1]] for a in range(1, 8)]
    n_rows = []
    for a in range(PEER_TOPK):
        cs = rows[a] if a < 8 else [cands[9][a - 8:a - 7]]
        gt = functools.reduce(jnp.add, [jnp.sum(jnp.where(c > tau, 1.0, 0.0), axis=0, keepdims=True) for c in cs])
        eq = functools.reduce(jnp.add, [jnp.sum(jnp.where(c == tau, 1.0, 0.0), axis=0, keepdims=True) for c in cs])
        take = jnp.minimum(eq, jnp.maximum(need, 0.0))
        need = need - eq
        n_rows.append(gt + take)
    return jnp.concatenate(n_rows, axis=0)


def _mid_kernel(x_ref, oa_ref, ob_ref, woa_ref, wob_ref, gffn_ref, wqt_ref, sk_ref,
                x1_ref, xnt_ref, n1_ref, a_ref, r2_ref, b_ref):
    tm = x_ref.shape[0]
    x1 = (x_ref[...] + jnp.dot(oa_ref[...], woa_ref[...], preferred_element_type=F32)
          + jnp.dot(ob_ref[...], wob_ref[...], preferred_element_type=F32))
    x1_ref[...] = x1
    xn = x1 * _rms_scale(x1, D_MODEL) * gffn_ref[...]
    xnt = xn.T.astype(BF16)
    xnt_ref[...] = xnt
    qt = jnp.dot(wqt_ref[...], xnt, preferred_element_type=F32).astype(BF16)
    rowk = lax.broadcasted_iota(jnp.int32, (PEER_TOPK, tm), 0).astype(F32)
    for hh in range(PEER_HEADS):
        s, vals, rank = [], [], []
        for p in range(2):
            hp = 2 * hh + p
            sp = jnp.dot(sk_ref[hp], qt[hp * LANES:(hp + 1) * LANES, :], preferred_element_type=F32)
            vp, rp = _topk_ranks(sp)
            s.append(sp), vals.append(vp), rank.append(rp)
        n = _pair_counts(vals[0], vals[1])
        e1 = jnp.exp(vals[0] - vals[0][0:1])
        e2 = jnp.exp(vals[1] - vals[1][0:1])
        paired = jnp.zeros((PEER_TOPK, tm), F32)
        for a in range(PEER_TOPK):
            paired = paired + jnp.where(rowk < n[a:a + 1], e1[a:a + 1], 0.0)
        z = jnp.sum(paired * e2, axis=0, keepdims=True)
        n1 = jnp.zeros((PEER_N_KEYS, tm), F32)
        for a in range(PEER_TOPK):
            n1 = jnp.where(rank[0] == float(a), n[a:a + 1], n1)
        n1_ref[hh] = n1
        a_ref[hh] = jnp.exp(s[0] - vals[0][0:1]) / z
        r2_ref[hh] = rank[1].astype(BF16)
        b_ref[hh] = jnp.exp(s[1] - vals[1][0:1]).astype(BF16)


def _peer_kernel(n1_ref, a_ref, r2_ref, b_ref, xnt_ref, u_ref, vt_ref, x1_ref, y_ref, acc_ref, g_ref):
    e = pl.program_id(1)
    tt = xnt_ref.shape[1]
    keys = PEER_SUB // PEER_N_KEYS
    pack = 16
    reps = (PEER_N_KEYS // pack, 1)

    @pl.when(e == 0)
    def _():
        acc_ref[...] = jnp.zeros_like(acc_ref)

    def weights(k):
        w = None
        for hh in range(PEER_HEADS):
            n16 = jnp.broadcast_to(n1_ref[hh, k:k + 1, :], (pack, tt)).astype(BF16)
            a16 = jnp.broadcast_to(a_ref[hh, k:k + 1, :], (pack, tt)).astype(BF16)
            term = jnp.where(r2_ref[hh] < jnp.tile(n16, reps), b_ref[hh], jnp.zeros((), BF16)) * jnp.tile(a16, reps)
            w = term if w is None else w + term
        return w

    def activations(c):
        rows = slice(c * PEER_SUB, (c + 1) * PEER_SUB)
        at = jnp.dot(u_ref[rows, :], xnt_ref[...], preferred_element_type=F32)
        g_ref[c % 2] = jax.nn.gelu(at).astype(BF16)

    def accumulate(c):
        rows = slice(c * PEER_SUB, (c + 1) * PEER_SUB)
        wa = jnp.concatenate([g_ref[c % 2, k * PEER_N_KEYS:(k + 1) * PEER_N_KEYS, :] * weights(c * keys + k)
                              for k in range(keys)], axis=0)
        acc_ref[...] += jnp.dot(vt_ref[:, rows], wa, preferred_element_type=F32)

    n_sub = EC_PEER // PEER_SUB
    activations(0)
    for c in range(n_sub):
        if c + 1 < n_sub:
            activations(c + 1)
        accumulate(c)

    @pl.when(e == pl.num_programs(1) - 1)
    def _():
        y_ref[...] = x1_ref[...] + acc_ref[...].T


def _const_spec(shape):
    zeros = (0,) * len(shape)
    return pl.BlockSpec(shape, lambda *_: zeros)


def _params(semantics):
    return pltpu.CompilerParams(dimension_semantics=semantics, vmem_limit_bytes=VMEM_LIMIT)


def kernel(x, norm_mix_g, w_in, gmlp_vnorm_g, gmlp_ws, gmlp_bs, mla_qnorm_g, mla_w_uq, mla_kvnorm_g, mla_w_ukv,
           mla_qk_q_g, mla_qk_k_g, w_out, norm_ffn_g, peer_w_q, peer_subkeys, peer_u, peer_v):
    bsz, seq, d = x.shape
    assert bsz == 1 and d == D_MODEL and seq % TQ_ATTN == 0 and seq % TT_PEER == 0
    assert peer_u.shape[1] % ((MLA_HEADS // ATTN_HEADS) * (seq // TQ_ATTN) * LANES) == 0
    x2 = x[0]
    n_exp = peer_u.shape[1]

    rot = (np.arange(MLA_ROPE) + MLA_ROPE // 2) % MLA_ROPE
    c_rope = 2 * GMLP_WIDTH + MLA_Q_RANK + MLA_KV_RANK
    w_rope = w_in[0][:, c_rope:]
    w_in_ext = jnp.concatenate([w_in[0][:, :c_rope], w_rope, w_rope, w_rope[:, rot], w_rope[:, rot]],
                               axis=1).astype(BF16)
    wq3 = mla_w_uq[0].reshape(MLA_Q_RANK, MLA_HEADS, MLA_QK)
    wq_rope = wq3[:, :, MLA_NOPE:]
    w_uq_ext = jnp.concatenate([wq3[:, :, :MLA_NOPE].reshape(MLA_Q_RANK, -1),
                                wq_rope.reshape(MLA_Q_RANK, -1),
                                wq_rope[:, :, rot].reshape(MLA_Q_RANK, -1)], axis=1).astype(BF16)
    wkv3 = mla_w_ukv[0].reshape(MLA_KV_RANK, MLA_HEADS, MLA_NOPE + LANES)
    w_ukv_ext = jnp.concatenate([wkv3[:, :, :MLA_NOPE].reshape(MLA_KV_RANK, -1),
                                 wkv3[:, :, MLA_NOPE:].reshape(MLA_KV_RANK, -1)], axis=1).astype(BF16)
    row = lambda v: v.reshape(1, -1)
    gq, gk = mla_qk_q_g[0], mla_qk_k_g[0]
    dup = lambda v: row(jnp.concatenate([v, v]))
    bsb = jnp.repeat(gmlp_bs[0].T, LANES, axis=1)
    pos = jnp.arange(seq, dtype=F32)
    inv_freq = ROPE_THETA ** (-jnp.arange(0, MLA_ROPE, 2, dtype=F32) / MLA_ROPE)
    ang = pos[:, None] * inv_freq[None, :]
    cos, sin = jnp.cos(ang), jnp.sin(ang)
    cos_f = jnp.concatenate([cos, cos, cos, cos], axis=1)
    sin_f = jnp.concatenate([-sin, sin, -sin, sin], axis=1)

    n_front = seq // TM_FRONT
    rows = lambda w: pl.BlockSpec((TM_FRONT, w), lambda i: (i, 0))
    heads = lambda w: pl.BlockSpec((MLA_HEADS, TM_FRONT, w), lambda i: (0, i, 0))
    oa, q, k, v = pl.pallas_call(
        _front_kernel,
        grid=(n_front,),
        in_specs=[rows(D_MODEL), _const_spec((1, D_MODEL)), _const_spec(w_in_ext.shape),
                  _const_spec((1, GMLP_WIDTH)), _const_spec(gmlp_ws[0].shape), _const_spec(bsb.shape),
                  _const_spec((1, MLA_Q_RANK)), _const_spec(w_uq_ext.shape),
                  _const_spec((1, MLA_KV_RANK)), _const_spec(w_ukv_ext.shape),
                  _const_spec((1, LANES)), _const_spec((1, LANES)), _const_spec((1, LANES)),
                  _const_spec((1, LANES)), _const_spec((1, LANES)), _const_spec((1, LANES)),
                  rows(LANES), rows(LANES)],
        out_specs=[rows(GMLP_WIDTH), heads(QK_PAD), heads(QK_PAD), heads(LANES)],
        out_shape=[jax.ShapeDtypeStruct((seq, GMLP_WIDTH), BF16),
                   jax.ShapeDtypeStruct((MLA_HEADS, seq, QK_PAD), BF16),
                   jax.ShapeDtypeStruct((MLA_HEADS, seq, QK_PAD), BF16),
                   jax.ShapeDtypeStruct((MLA_HEADS, seq, LANES), BF16)],
        compiler_params=_params(("parallel",)),
        name="front",
    )(x2, row(norm_mix_g[0]), w_in_ext, row(gmlp_vnorm_g[0]), gmlp_ws[0], bsb,
      row(mla_qnorm_g[0]), w_uq_ext, row(mla_kvnorm_g[0]), w_ukv_ext,
      row(gq[:MLA_NOPE]), dup(gq[MLA_NOPE:]), dup(gq[MLA_NOPE:][rot]),
      row(gk[:MLA_NOPE]), dup(gk[MLA_NOPE:]), dup(gk[MLA_NOPE:][rot]), cos_f, sin_f)

    n_q = seq // TQ_ATTN
    n_attn_steps = (MLA_HEADS // ATTN_HEADS) * n_q
    slab = n_exp // n_attn_steps
    ob, u_bf, vt_bf = pl.pallas_call(
        _attn_kernel,
        grid=(MLA_HEADS // ATTN_HEADS, n_q),
        in_specs=[pl.BlockSpec((ATTN_HEADS, TQ_ATTN, QK_PAD), lambda h, i: (h, i, 0)),
                  pl.BlockSpec((ATTN_HEADS, seq, QK_PAD), lambda h, i: (h, 0, 0)),
                  pl.BlockSpec((ATTN_HEADS, seq, LANES), lambda h, i: (h, 0, 0)),
                  pl.BlockSpec((slab, D_MODEL), lambda h, i: (h * n_q + i, 0)),
                  pl.BlockSpec((slab, D_MODEL), lambda h, i: (h * n_q + i, 0))],
        out_specs=[pl.BlockSpec((TQ_ATTN, ATTN_HEADS * LANES), lambda h, i: (i, h)),
                   pl.BlockSpec((slab, D_MODEL), lambda h, i: (h * n_q + i, 0)),
                   pl.BlockSpec((D_MODEL, slab), lambda h, i: (0, h * n_q + i))],
        out_shape=[jax.ShapeDtypeStruct((seq, MLA_HEADS * LANES), BF16),
                   jax.ShapeDtypeStruct((n_exp, D_MODEL), BF16),
                   jax.ShapeDtypeStruct((D_MODEL, n_exp), BF16)],
        compiler_params=_params(("parallel", "parallel")),
        name="attn",
    )(q, k, v, peer_u[0], peer_v[0])

    wo = w_out[0].astype(BF16)
    wqt = peer_w_q[0].T.astype(BF16)
    sk = peer_subkeys[0].reshape(PEER_HEADS * 2, PEER_N_KEYS, -1).astype(BF16)
    n_mid = seq // TM_MID
    mrows = lambda w: pl.BlockSpec((TM_MID, w), lambda i: (i, 0))
    tok = pl.BlockSpec((PEER_HEADS, PEER_N_KEYS, TM_MID), lambda i: (0, 0, i))
    tok_shape = lambda dt: jax.ShapeDtypeStruct((PEER_HEADS, PEER_N_KEYS, seq), dt)
    x1, xnt, n1, a1, r2, b2 = pl.pallas_call(
        _mid_kernel,
        grid=(n_mid,),
        in_specs=[mrows(D_MODEL), mrows(GMLP_WIDTH), mrows(MLA_HEADS * LANES),
                  _const_spec((GMLP_WIDTH, D_MODEL)), _const_spec((MLA_HEADS * LANES, D_MODEL)),
                  _const_spec((1, D_MODEL)), _const_spec(wqt.shape), _const_spec(sk.shape)],
        out_specs=[mrows(D_MODEL), pl.BlockSpec((D_MODEL, TM_MID), lambda i: (0, i)), tok, tok, tok, tok],
        out_shape=[jax.ShapeDtypeStruct((seq, D_MODEL), F32), jax.ShapeDtypeStruct((D_MODEL, seq), BF16),
                   tok_shape(F32), tok_shape(F32), tok_shape(BF16), tok_shape(BF16)],
        compiler_params=_params(("parallel",)),
        name="mid",
    )(x2, oa, ob, wo[:GMLP_WIDTH], wo[GMLP_WIDTH:], row(norm_ffn_g[0]), wqt, sk)

    keys_per_step = EC_PEER // PEER_N_KEYS
    first_half = pl.BlockSpec((PEER_HEADS, keys_per_step, TT_PEER), lambda t, e: (0, e, t))
    second_half = pl.BlockSpec((PEER_HEADS, PEER_N_KEYS, TT_PEER), lambda t, e: (0, 0, t))
    y = pl.pallas_call(
        _peer_kernel,
        grid=(seq // TT_PEER, n_exp // EC_PEER),
        in_specs=[first_half, first_half, second_half, second_half,
                  pl.BlockSpec((D_MODEL, TT_PEER), lambda t, e: (0, t)),
                  pl.BlockSpec((EC_PEER, D_MODEL), lambda t, e: (e, 0)),
                  pl.BlockSpec((D_MODEL, EC_PEER), lambda t, e: (0, e)),
                  pl.BlockSpec((TT_PEER, D_MODEL), lambda t, e: (t, 0))],
        out_specs=pl.BlockSpec((TT_PEER, D_MODEL), lambda t, e: (t, 0)),
        out_shape=jax.ShapeDtypeStruct((seq, D_MODEL), F32),
        scratch_shapes=[pltpu.VMEM((D_MODEL, TT_PEER), F32), pltpu.VMEM((2, PEER_SUB, TT_PEER), BF16)],
        compiler_params=_params(("parallel", "arbitrary")),
        name="peer",
    )(n1, a1, r2, b2, xnt, u_bf, vt_bf, x1)
    return y[None]
```

```python
import functools

import jax
import jax.numpy as jnp
import numpy as np
from jax import lax
from jax.experimental import pallas as pl
from jax.experimental.pallas import tpu as pltpu

F32 = jnp.float32
BF16 = jnp.bfloat16

D_MODEL = 2048
CHUNK = 64
GMLP_HEADS = 8
GMLP_WIDTH = 1024
GMLP_BLOCK = 128
MLA_HEADS = 8
MLA_NOPE = 128
MLA_ROPE = 64
MLA_QK = MLA_NOPE + MLA_ROPE
MLA_Q_RANK = 512
MLA_KV_RANK = 256
ROPE_THETA = 10000.0
PEER_HEADS = 8
PEER_N_KEYS = 128
PEER_TOPK = 16
RMS_EPS = 1e-6
LANES = 128
QK_PAD = 2 * LANES

TM_FRONT = 256
TQ_ATTN = 512
TK_ATTN = 512
ATTN_HEADS = 2
TM_MID = 256
TT_PEER = 512
EC_PEER = 1024
PEER_SUB = 256
VMEM_LIMIT = 56 * 1024 * 1024


def _rms_scale(v, width):
    return lax.rsqrt(jnp.sum(v * v, axis=-1, keepdims=True) * (1.0 / width) + RMS_EPS)


def _front_kernel(x_ref, gmix_ref, win_ref, gv_ref, ws_ref, bsb_ref, gq_ref, wuq_ref, gkv_ref, wukv_ref,
                  gqn_ref, gqr_ref, gqrr_ref, gkn_ref, gkr_ref, gkrr_ref, cos_ref, sin_ref,
                  oa_ref, q_ref, k_ref, v_ref):
    tm = x_ref.shape[0]
    x = x_ref[...]
    h = x * _rms_scale(x, D_MODEL) * gmix_ref[...]
    proj = jnp.dot(h.astype(BF16), win_ref[...], preferred_element_type=F32)

    zu = jax.nn.gelu(proj[:, :GMLP_WIDTH])
    zv = jax.nn.gelu(proj[:, GMLP_WIDTH:2 * GMLP_WIDTH])
    vn = (zv * _rms_scale(zv, GMLP_WIDTH) * gv_ref[...]).astype(BF16)
    row = lax.broadcasted_iota(jnp.int32, (GMLP_BLOCK, GMLP_BLOCK), 0)
    col = lax.broadcasted_iota(jnp.int32, (GMLP_BLOCK, GMLP_BLOCK), 1)
    tri = col <= row
    for hh in range(GMLP_HEADS):
        w = jnp.where(tri, ws_ref[hh], 0.0).astype(BF16)
        cs = slice(hh * LANES, (hh + 1) * LANES)
        for r in range(tm // GMLP_BLOCK):
            rs = slice(r * GMLP_BLOCK, (r + 1) * GMLP_BLOCK)
            sv = jnp.dot(w, vn[rs, cs], preferred_element_type=F32) + bsb_ref[:, cs]
            oa_ref[rs, cs] = (zu[rs, cs] * sv).astype(BF16)

    c0 = 2 * GMLP_WIDTH
    cq = proj[:, c0:c0 + MLA_Q_RANK]
    ckv = proj[:, c0 + MLA_Q_RANK:c0 + MLA_Q_RANK + MLA_KV_RANK]
    c1 = c0 + MLA_Q_RANK + MLA_KV_RANK
    kr = proj[:, c1:c1 + LANES]
    krr = proj[:, c1 + LANES:c1 + 2 * LANES]
    cqn = (cq * _rms_scale(cq, MLA_Q_RANK) * gq_ref[...]).astype(BF16)
    qall = jnp.dot(cqn, wuq_ref[...], preferred_element_type=F32)
    ckvn = (ckv * _rms_scale(ckv, MLA_KV_RANK) * gkv_ref[...]).astype(BF16)
    kvall = jnp.dot(ckvn, wukv_ref[...], preferred_element_type=F32)

    cos_f = cos_ref[...]
    sin_f = sin_ref[...]
    lane = lax.broadcasted_iota(jnp.int32, (tm, LANES), 1)
    lo = lane < MLA_ROPE
    k_rot = kr * gkr_ref[...] * cos_f + krr * gkrr_ref[...] * sin_f
    ssq_kr = jnp.sum(jnp.where(lo, kr * kr, 0.0), axis=-1, keepdims=True)
    q_scale = MLA_QK ** -0.5
    nh = MLA_HEADS * MLA_NOPE
    for g in range(MLA_HEADS // 2):
        qr = qall[:, nh + g * LANES:nh + (g + 1) * LANES]
        qrr = qall[:, nh + 512 + g * LANES:nh + 512 + (g + 1) * LANES]
        q_rot = qr * gqr_ref[...] * cos_f + qrr * gqrr_ref[...] * sin_f
        qr2 = qr * qr
        ssq_half = (jnp.sum(jnp.where(lo, qr2, 0.0), axis=-1, keepdims=True),
                    jnp.sum(jnp.where(lo, 0.0, qr2), axis=-1, keepdims=True))
        for p in range(2):
            hh = 2 * g + p
            hs = slice(hh * LANES, (hh + 1) * LANES)
            qn = qall[:, hs]
            rinv = lax.rsqrt((jnp.sum(qn * qn, axis=-1, keepdims=True) + ssq_half[p]) * (1.0 / MLA_QK)
                             + RMS_EPS) * q_scale
            mine = lo if p == 0 else jnp.logical_not(lo)
            q_ref[hh, :, :LANES] = (qn * gqn_ref[...] * rinv).astype(BF16)
            q_ref[hh, :, LANES:] = (jnp.where(mine, q_rot, 0.0) * rinv).astype(BF16)
            kn = kvall[:, hs]
            rinvk = lax.rsqrt((jnp.sum(kn * kn, axis=-1, keepdims=True) + ssq_kr) * (1.0 / MLA_QK) + RMS_EPS)
            k_ref[hh, :, :LANES] = (kn * gkn_ref[...] * rinvk).astype(BF16)
            k_ref[hh, :, LANES:] = (k_rot * rinvk).astype(BF16)
            v_ref[hh] = kvall[:, nh + hh * LANES:nh + (hh + 1) * LANES].astype(BF16)


def _attn_kernel(q_ref, k_ref, v_ref, u_ref, pv_ref, o_ref, ub_ref, vt_ref):
    tq = q_ref.shape[1]
    qi = pl.program_id(1)
    ub_ref[...] = u_ref[...].astype(BF16)
    vt_ref[...] = pv_ref[...].T.astype(BF16)
    qs = [q_ref[j] for j in range(ATTN_HEADS)]

    def tile(kt, carry, masked):
        start = pl.multiple_of(kt * TK_ATTN, TK_ATTN)
        out = []
        for j in range(ATTN_HEADS):
            m, l, acc = carry[j]
            kk = k_ref[j, pl.ds(start, TK_ATTN), :]
            vv = v_ref[j, pl.ds(start, TK_ATTN), :]
            s = lax.dot_general(qs[j], kk, (((1,), (1,)), ((), ())), preferred_element_type=F32)
            if masked:
                r_chunk = lax.broadcasted_iota(jnp.int32, (tq, TK_ATTN), 0) // CHUNK
                c_chunk = lax.broadcasted_iota(jnp.int32, (tq, TK_ATTN), 1) // CHUNK
                s = jnp.where(c_chunk <= r_chunk, s, -jnp.inf)
            m_new = jnp.maximum(m, jnp.max(s, axis=-1, keepdims=True))
            alpha = jnp.exp(m - m_new)
            p = jnp.exp(s - m_new)
            l = alpha * l + jnp.sum(p, axis=-1, keepdims=True)
            acc = alpha * acc + jnp.dot(p.astype(BF16), vv, preferred_element_type=F32)
            out.append((m_new, l, acc))
        return tuple(out)

    init = tuple((jnp.full((tq, 1), -jnp.inf, F32), jnp.zeros((tq, 1), F32), jnp.zeros((tq, LANES), F32))
                 for _ in range(ATTN_HEADS))
    carry = lax.fori_loop(0, qi, lambda kt, c: tile(kt, c, False), init)
    carry = tile(qi, carry, True)
    for j in range(ATTN_HEADS):
        _, l, acc = carry[j]
        o_ref[:, j * LANES:(j + 1) * LANES] = (acc / l).astype(BF16)


def _topk_ranks(s):
    n = s.shape[0]
    idx = lax.broadcasted_iota(jnp.int32, s.shape, 0)
    rank = jnp.full(s.shape, float(PEER_TOPK), F32)
    vals = []
    for kk in range(PEER_TOPK):
        m = jnp.max(s, axis=0, keepdims=True)
        first = jnp.min(jnp.where(s == m, idx, n), axis=0, keepdims=True)
        sel = idx == first
        rank = jnp.where(sel, float(kk), rank)
        s = jnp.where(sel, -jnp.inf, s)
        vals.append(m)
    return jnp.concatenate(vals, axis=0), rank


def _topk_fast(s):
    rank = jnp.full(s.shape, float(PEER_TOPK), F32)
    vals = []
    for kk in range(PEER_TOPK):
        m = jnp.max(s, axis=0, keepdims=True)
        hit = s == m
        rank = jnp.where(hit, float(kk), rank)
        s = jnp.where(hit, -jnp.inf, s)
        vals.append(m)
    taken = jnp.sum(jnp.where(rank < PEER_TOPK, 1.0, 0.0), axis=0, keepdims=True)
    return jnp.concatenate(vals, axis=0), rank, taken


def _pair_counts(v1, v2):
    t = v1.shape[1]
    row8 = lax.broadcasted_iota(jnp.int32, (8, t), 0)
    cands = [v1[0:1] + v2[0:8], v1[0:1] + v2[8:16]]
    for a in range(1, 8):
        cands.append(jnp.where(row8 < PEER_TOPK // (a + 1), v1[a:a + 1] + v2[0:8], -jnp.inf))
    cands.append(v1[8:16] + v2[0:1])
    work = list(cands)
    cum = jnp.zeros((1, t), F32)
    tau = jnp.zeros((1, t), F32)
    above = jnp.zeros((1, t), F32)
    for _ in range(PEER_TOPK):
        m = jnp.max(functools.reduce(jnp.maximum, work), axis=0, keepdims=True)
        hit = [c == m for c in work]
        cnt = jnp.sum(functools.reduce(jnp.add, [jnp.where(hh, 1.0, 0.0) for hh in hit]), axis=0, keepdims=True)
        active = cum < PEER_TOPK
        tau = jnp.where(active, m, tau)
        above = jnp.where(active, cum, above)
        cum = cum + cnt
        work = [jnp.where(hh, -jnp.inf, c) for hh, c in zip(hit, work)]
    need = PEER_TOPK - above
    rows = [[cands[0], cands[1]]] + [[cands[a + 1]] for a in range(1, 8)]
    n_rows = []
    for a in range(PEER_TOPK):
        cs = rows[a] if a < 8 else [cands[9][a - 8:a - 7]]
        gt = functools.reduce(jnp.add, [jnp.sum(jnp.where(c > tau, 1.0, 0.0), axis=0, keepdims=True) for c in cs])
        eq = functools.reduce(jnp.add, [jnp.sum(jnp.where(c == tau, 1.0, 0.0), axis=0, keepdims=True) for c in cs])
        take = jnp.minimum(eq, jnp.maximum(need, 0.0))
        need = need - eq
        n_rows.append(gt + take)
    return jnp.concatenate(n_rows, axis=0)


def _mid_kernel(x_ref, oa_ref, ob_ref, woa_ref, wob_ref, gffn_ref, wqt_ref, sk_ref,
                x1_ref, xnt_ref, n1_ref, a_ref, r2_ref, b_ref, s_ref, vals_ref, rank_ref):
    tm = x_ref.shape[0]
    x1 = (x_ref[...] + jnp.dot(oa_ref[...], woa_ref[...], preferred_element_type=F32)
          + jnp.dot(ob_ref[...], wob_ref[...], preferred_element_type=F32))
    x1_ref[...] = x1
    xn = x1 * _rms_scale(x1, D_MODEL) * gffn_ref[...]
    xnt = xn.T.astype(BF16)
    xnt_ref[...] = xnt
    qt = jnp.dot(wqt_ref[...], xnt, preferred_element_type=F32).astype(BF16)

    n_hp = 2 * PEER_HEADS
    excess = jnp.zeros((1, tm), F32)
    for hp in range(n_hp):
        sp = jnp.dot(sk_ref[hp], qt[hp * LANES:(hp + 1) * LANES, :], preferred_element_type=F32)
        s_ref[hp] = sp
        vp, rp, taken = _topk_fast(sp)
        vals_ref[hp] = vp
        rank_ref[hp] = rp
        excess = jnp.maximum(excess, taken - PEER_TOPK)

    @pl.when(jnp.max(excess) > 0.0)
    def _():
        for hp in range(n_hp):
            vp, rp = _topk_ranks(s_ref[hp])
            vals_ref[hp] = vp
            rank_ref[hp] = rp

    rowk = lax.broadcasted_iota(jnp.int32, (PEER_TOPK, tm), 0).astype(F32)
    for hh in range(PEER_HEADS):
        v1, v2 = vals_ref[2 * hh], vals_ref[2 * hh + 1]
        n = _pair_counts(v1, v2)
        e1 = jnp.exp(v1 - v1[0:1])
        e2 = jnp.exp(v2 - v2[0:1])
        paired = jnp.zeros((PEER_TOPK, tm), F32)
        for a in range(PEER_TOPK):
            paired = paired + jnp.where(rowk < n[a:a + 1], e1[a:a + 1], 0.0)
        z = jnp.sum(paired * e2, axis=0, keepdims=True)
        rank1 = rank_ref[2 * hh]
        n1 = jnp.zeros((PEER_N_KEYS, tm), F32)
        for a in range(PEER_TOPK):
            n1 = jnp.where(rank1 == float(a), n[a:a + 1], n1)
        n1_ref[hh] = n1
        a_ref[hh] = jnp.exp(s_ref[2 * hh] - v1[0:1]) / z
        r2_ref[hh] = rank_ref[2 * hh + 1].astype(BF16)
        b_ref[hh] = jnp.exp(s_ref[2 * hh + 1] - v2[0:1]).astype(BF16)


def _peer_kernel(n1_ref, a_ref, r2_ref, b_ref, xnt_ref, u_ref, vt_ref, x1_ref, y_ref, acc_ref, g_ref):
    e = pl.program_id(1)
    tt = xnt_ref.shape[1]
    keys = PEER_SUB // PEER_N_KEYS
    pack = 16
    reps = (PEER_N_KEYS // pack, 1)

    @pl.when(e == 0)
    def _():
        acc_ref[...] = jnp.zeros_like(acc_ref)

    def weights(k):
        w = None
        for hh in range(PEER_HEADS):
            n16 = jnp.broadcast_to(n1_ref[hh, k:k + 1, :], (pack, tt)).astype(BF16)
            a16 = jnp.broadcast_to(a_ref[hh, k:k + 1, :], (pack, tt)).astype(BF16)
            term = jnp.where(r2_ref[hh] < jnp.tile(n16, reps), b_ref[hh], jnp.zeros((), BF16)) * jnp.tile(a16, reps)
            w = term if w is None else w + term
        return w

    def activations(c):
        rows = slice(c * PEER_SUB, (c + 1) * PEER_SUB)
        at = jnp.dot(u_ref[rows, :], xnt_ref[...], preferred_element_type=F32)
        g_ref[c % 2] = jax.nn.gelu(at).astype(BF16)

    def accumulate(c):
        rows = slice(c * PEER_SUB, (c + 1) * PEER_SUB)
        wa = jnp.concatenate([g_ref[c % 2, k * PEER_N_KEYS:(k + 1) * PEER_N_KEYS, :] * weights(c * keys + k)
                              for k in range(keys)], axis=0)
        acc_ref[...] += jnp.dot(vt_ref[:, rows], wa, preferred_element_type=F32)

    n_sub = EC_PEER // PEER_SUB
    activations(0)
    for c in range(n_sub):
        if c + 1 < n_sub:
            activations(c + 1)
        accumulate(c)

    @pl.when(e == pl.num_programs(1) - 1)
    def _():
        y_ref[...] = x1_ref[...] + acc_ref[...].T


def _const_spec(shape):
    zeros = (0,) * len(shape)
    return pl.BlockSpec(shape, lambda *_: zeros)


def _params(semantics):
    return pltpu.CompilerParams(dimension_semantics=semantics, vmem_limit_bytes=VMEM_LIMIT)


def kernel(x, norm_mix_g, w_in, gmlp_vnorm_g, gmlp_ws, gmlp_bs, mla_qnorm_g, mla_w_uq, mla_kvnorm_g, mla_w_ukv,
           mla_qk_q_g, mla_qk_k_g, w_out, norm_ffn_g, peer_w_q, peer_subkeys, peer_u, peer_v):
    bsz, seq, d = x.shape
    assert bsz == 1 and d == D_MODEL and seq % TQ_ATTN == 0 and seq % TT_PEER == 0
    assert peer_u.shape[1] % ((MLA_HEADS // ATTN_HEADS) * (seq // TQ_ATTN) * LANES) == 0
    x2 = x[0]
    n_exp = peer_u.shape[1]

    rot = (np.arange(MLA_ROPE) + MLA_ROPE // 2) % MLA_ROPE
    c_rope = 2 * GMLP_WIDTH + MLA_Q_RANK + MLA_KV_RANK
    w_rope = w_in[0][:, c_rope:]
    w_in_ext = jnp.concatenate([w_in[0][:, :c_rope], w_rope, w_rope, w_rope[:, rot], w_rope[:, rot]],
                               axis=1).astype(BF16)
    wq3 = mla_w_uq[0].reshape(MLA_Q_RANK, MLA_HEADS, MLA_QK)
    wq_rope = wq3[:, :, MLA_NOPE:]
    w_uq_ext = jnp.concatenate([wq3[:, :, :MLA_NOPE].reshape(MLA_Q_RANK, -1),
                                wq_rope.reshape(MLA_Q_RANK, -1),
                                wq_rope[:, :, rot].reshape(MLA_Q_RANK, -1)], axis=1).astype(BF16)
    wkv3 = mla_w_ukv[0].reshape(MLA_KV_RANK, MLA_HEADS, MLA_NOPE + LANES)
    w_ukv_ext = jnp.concatenate([wkv3[:, :, :MLA_NOPE].reshape(MLA_KV_RANK, -1),
                                 wkv3[:, :, MLA_NOPE:].reshape(MLA_KV_RANK, -1)], axis=1).astype(BF16)
    row = lambda v: v.reshape(1, -1)
    gq, gk = mla_qk_q_g[0], mla_qk_k_g[0]
    dup = lambda v: row(jnp.concatenate([v, v]))
    bsb = jnp.repeat(gmlp_bs[0].T, LANES, axis=1)
    pos = jnp.arange(seq, dtype=F32)
    inv_freq = ROPE_THETA ** (-jnp.arange(0, MLA_ROPE, 2, dtype=F32) / MLA_ROPE)
    ang = pos[:, None] * inv_freq[None, :]
    cos, sin = jnp.cos(ang), jnp.sin(ang)
    cos_f = jnp.concatenate([cos, cos, cos, cos], axis=1)
    sin_f = jnp.concatenate([-sin, sin, -sin, sin], axis=1)

    n_front = seq // TM_FRONT
    rows = lambda w: pl.BlockSpec((TM_FRONT, w), lambda i: (i, 0))
    heads = lambda w: pl.BlockSpec((MLA_HEADS, TM_FRONT, w), lambda i: (0, i, 0))
    oa, q, k, v = pl.pallas_call(
        _front_kernel,
        grid=(n_front,),
        in_specs=[rows(D_MODEL), _const_spec((1, D_MODEL)), _const_spec(w_in_ext.shape),
                  _const_spec((1, GMLP_WIDTH)), _const_spec(gmlp_ws[0].shape), _const_spec(bsb.shape),
                  _const_spec((1, MLA_Q_RANK)), _const_spec(w_uq_ext.shape),
                  _const_spec((1, MLA_KV_RANK)), _const_spec(w_ukv_ext.shape),
                  _const_spec((1, LANES)), _const_spec((1, LANES)), _const_spec((1, LANES)),
                  _const_spec((1, LANES)), _const_spec((1, LANES)), _const_spec((1, LANES)),
                  rows(LANES), rows(LANES)],
        out_specs=[rows(GMLP_WIDTH), heads(QK_PAD), heads(QK_PAD), heads(LANES)],
        out_shape=[jax.ShapeDtypeStruct((seq, GMLP_WIDTH), BF16),
                   jax.ShapeDtypeStruct((MLA_HEADS, seq, QK_PAD), BF16),
                   jax.ShapeDtypeStruct((MLA_HEADS, seq, QK_PAD), BF16),
                   jax.ShapeDtypeStruct((MLA_HEADS, seq, LANES), BF16)],
        compiler_params=_params(("parallel",)),
        name="front",
    )(x2, row(norm_mix_g[0]), w_in_ext, row(gmlp_vnorm_g[0]), gmlp_ws[0], bsb,
      row(mla_qnorm_g[0]), w_uq_ext, row(mla_kvnorm_g[0]), w_ukv_ext,
      row(gq[:MLA_NOPE]), dup(gq[MLA_NOPE:]), dup(gq[MLA_NOPE:][rot]),
      row(gk[:MLA_NOPE]), dup(gk[MLA_NOPE:]), dup(gk[MLA_NOPE:][rot]), cos_f, sin_f)

    n_q = seq // TQ_ATTN
    n_attn_steps = (MLA_HEADS // ATTN_HEADS) * n_q
    slab = n_exp // n_attn_steps
    ob, u_bf, vt_bf = pl.pallas_call(
        _attn_kernel,
        grid=(MLA_HEADS // ATTN_HEADS, n_q),
        in_specs=[pl.BlockSpec((ATTN_HEADS, TQ_ATTN, QK_PAD), lambda h, i: (h, i, 0)),
                  pl.BlockSpec((ATTN_HEADS, seq, QK_PAD), lambda h, i: (h, 0, 0)),
                  pl.BlockSpec((ATTN_HEADS, seq, LANES), lambda h, i: (h, 0, 0)),
                  pl.BlockSpec((slab, D_MODEL), lambda h, i: (h * n_q + i, 0)),
                  pl.BlockSpec((slab, D_MODEL), lambda h, i: (h * n_q + i, 0))],
        out_specs=[pl.BlockSpec((TQ_ATTN, ATTN_HEADS * LANES), lambda h, i: (i, h)),
                   pl.BlockSpec((slab, D_MODEL), lambda h, i: (h * n_q + i, 0)),
                   pl.BlockSpec((D_MODEL, slab), lambda h, i: (0, h * n_q + i))],
        out_shape=[jax.ShapeDtypeStruct((seq, MLA_HEADS * LANES), BF16),
                   jax.ShapeDtypeStruct((n_exp, D_MODEL), BF16),
                   jax.ShapeDtypeStruct((D_MODEL, n_exp), BF16)],
        compiler_params=_params(("parallel", "parallel")),
        name="attn",
    )(q, k, v, peer_u[0], peer_v[0])

    wo = w_out[0].astype(BF16)
    wqt = peer_w_q[0].T.astype(BF16)
    sk = peer_subkeys[0].reshape(PEER_HEADS * 2, PEER_N_KEYS, -1).astype(BF16)
    n_mid = seq // TM_MID
    mrows = lambda w: pl.BlockSpec((TM_MID, w), lambda i: (i, 0))
    tok = pl.BlockSpec((PEER_HEADS, PEER_N_KEYS, TM_MID), lambda i: (0, 0, i))
    tok_shape = lambda dt: jax.ShapeDtypeStruct((PEER_HEADS, PEER_N_KEYS, seq), dt)
    x1, xnt, n1, a1, r2, b2 = pl.pallas_call(
        _mid_kernel,
        grid=(n_mid,),
        in_specs=[mrows(D_MODEL), mrows(GMLP_WIDTH), mrows(MLA_HEADS * LANES),
                  _const_spec((GMLP_WIDTH, D_MODEL)), _const_spec((MLA_HEADS * LANES, D_MODEL)),
                  _const_spec((1, D_MODEL)), _const_spec(wqt.shape), _const_spec(sk.shape)],
        out_specs=[mrows(D_MODEL), pl.BlockSpec((D_MODEL, TM_MID), lambda i: (0, i)), tok, tok, tok, tok],
        out_shape=[jax.ShapeDtypeStruct((seq, D_MODEL), F32), jax.ShapeDtypeStruct((D_MODEL, seq), BF16),
                   tok_shape(F32), tok_shape(F32), tok_shape(BF16), tok_shape(BF16)],
        scratch_shapes=[pltpu.VMEM((2 * PEER_HEADS, PEER_N_KEYS, TM_MID), F32),
                        pltpu.VMEM((2 * PEER_HEADS, PEER_TOPK, TM_MID), F32),
                        pltpu.VMEM((2 * PEER_HEADS, PEER_N_KEYS, TM_MID), F32)],
        compiler_params=_params(("parallel",)),
        name="mid",
    )(x2, oa, ob, wo[:GMLP_WIDTH], wo[GMLP_WIDTH:], row(norm_ffn_g[0]), wqt, sk)

    keys_per_step = EC_PEER // PEER_N_KEYS
    first_half = pl.BlockSpec((PEER_HEADS, keys_per_step, TT_PEER), lambda t, e: (0, e, t))
    second_half = pl.BlockSpec((PEER_HEADS, PEER_N_KEYS, TT_PEER), lambda t, e: (0, 0, t))
    y = pl.pallas_call(
        _peer_kernel,
        grid=(seq // TT_PEER, n_exp // EC_PEER),
        in_specs=[first_half, first_half, second_half, second_half,
                  pl.BlockSpec((D_MODEL, TT_PEER), lambda t, e: (0, t)),
                  pl.BlockSpec((EC_PEER, D_MODEL), lambda t, e: (e, 0)),
                  pl.BlockSpec((D_MODEL, EC_PEER), lambda t, e: (0, e)),
                  pl.BlockSpec((TT_PEER, D_MODEL), lambda t, e: (t, 0))],
        out_specs=pl.BlockSpec((TT_PEER, D_MODEL), lambda t, e: (t, 0)),
        out_shape=jax.ShapeDtypeStruct((seq, D_MODEL), F32),
        scratch_shapes=[pltpu.VMEM((D_MODEL, TT_PEER), F32), pltpu.VMEM((2, PEER_SUB, TT_PEER), BF16)],
        compiler_params=_params(("parallel", "arbitrary")),
        name="peer",
    )(n1, a1, r2, b2, xnt, u_bf, vt_bf, x1)
    return y[None]
```

```python
import functools

import jax
import jax.numpy as jnp
import numpy as np
from jax import lax
from jax.experimental import pallas as pl
from jax.experimental.pallas import tpu as pltpu

F32 = jnp.float32
BF16 = jnp.bfloat16

D_MODEL = 2048
CHUNK = 64
GMLP_HEADS = 8
GMLP_WIDTH = 1024
GMLP_BLOCK = 128
MLA_HEADS = 8
MLA_NOPE = 128
MLA_ROPE = 64
MLA_QK = MLA_NOPE + MLA_ROPE
MLA_Q_RANK = 512
MLA_KV_RANK = 256
ROPE_THETA = 10000.0
PEER_HEADS = 8
PEER_N_KEYS = 128
PEER_TOPK = 16
RMS_EPS = 1e-6
LANES = 128
QK_PAD = 2 * LANES

TM_FRONT = 256
TQ_ATTN = 512
TK_ATTN = 1024
ATTN_HEADS = 2
TM_MID = 256
TT_PEER = 512
EC_PEER = 1024
PEER_SUB = 512
VMEM_LIMIT = 56 * 1024 * 1024


def _rms_scale(v, width):
    return lax.rsqrt(jnp.sum(v * v, axis=-1, keepdims=True) * (1.0 / width) + RMS_EPS)


def _front_kernel(x_ref, gmix_ref, win_ref, gv_ref, ws_ref, bsb_ref, gq_ref, wuq_ref, gkv_ref, wukv_ref,
                  gqn_ref, gqr_ref, gqrr_ref, gkn_ref, gkr_ref, gkrr_ref, cos_ref, sin_ref,
                  oa_ref, q_ref, k_ref, v_ref):
    tm = x_ref.shape[0]
    x = x_ref[...]
    h = x * _rms_scale(x, D_MODEL) * gmix_ref[...]
    proj = jnp.dot(h.astype(BF16), win_ref[...], preferred_element_type=F32)

    zu = jax.nn.gelu(proj[:, :GMLP_WIDTH])
    zv = jax.nn.gelu(proj[:, GMLP_WIDTH:2 * GMLP_WIDTH])
    vn = (zv * _rms_scale(zv, GMLP_WIDTH) * gv_ref[...]).astype(BF16)
    row = lax.broadcasted_iota(jnp.int32, (GMLP_BLOCK, GMLP_BLOCK), 0)
    col = lax.broadcasted_iota(jnp.int32, (GMLP_BLOCK, GMLP_BLOCK), 1)
    tri = col <= row
    for hh in range(GMLP_HEADS):
        w = jnp.where(tri, ws_ref[hh], 0.0).astype(BF16)
        cs = slice(hh * LANES, (hh + 1) * LANES)
        for r in range(tm // GMLP_BLOCK):
            rs = slice(r * GMLP_BLOCK, (r + 1) * GMLP_BLOCK)
            sv = jnp.dot(w, vn[rs, cs], preferred_element_type=F32) + bsb_ref[:, cs]
            oa_ref[rs, cs] = (zu[rs, cs] * sv).astype(BF16)

    c0 = 2 * GMLP_WIDTH
    cq = proj[:, c0:c0 + MLA_Q_RANK]
    ckv = proj[:, c0 + MLA_Q_RANK:c0 + MLA_Q_RANK + MLA_KV_RANK]
    c1 = c0 + MLA_Q_RANK + MLA_KV_RANK
    kr = proj[:, c1:c1 + LANES]
    krr = proj[:, c1 + LANES:c1 + 2 * LANES]
    cqn = (cq * _rms_scale(cq, MLA_Q_RANK) * gq_ref[...]).astype(BF16)
    qall = jnp.dot(cqn, wuq_ref[...], preferred_element_type=F32)
    ckvn = (ckv * _rms_scale(ckv, MLA_KV_RANK) * gkv_ref[...]).astype(BF16)
    kvall = jnp.dot(ckvn, wukv_ref[...], preferred_element_type=F32)

    cos_f = cos_ref[...]
    sin_f = sin_ref[...]
    lane = lax.broadcasted_iota(jnp.int32, (tm, LANES), 1)
    lo = lane < MLA_ROPE
    k_rot = kr * gkr_ref[...] * cos_f + krr * gkrr_ref[...] * sin_f
    ssq_kr = jnp.sum(jnp.where(lo, kr * kr, 0.0), axis=-1, keepdims=True)
    q_scale = MLA_QK ** -0.5
    nh = MLA_HEADS * MLA_NOPE
    for g in range(MLA_HEADS // 2):
        qr = qall[:, nh + g * LANES:nh + (g + 1) * LANES]
        qrr = qall[:, nh + 512 + g * LANES:nh + 512 + (g + 1) * LANES]
        q_rot = qr * gqr_ref[...] * cos_f + qrr * gqrr_ref[...] * sin_f
        qr2 = qr * qr
        ssq_half = (jnp.sum(jnp.where(lo, qr2, 0.0), axis=-1, keepdims=True),
                    jnp.sum(jnp.where(lo, 0.0, qr2), axis=-1, keepdims=True))
        for p in range(2):
            hh = 2 * g + p
            hs = slice(hh * LANES, (hh + 1) * LANES)
            qn = qall[:, hs]
            rinv = lax.rsqrt((jnp.sum(qn * qn, axis=-1, keepdims=True) + ssq_half[p]) * (1.0 / MLA_QK)
                             + RMS_EPS) * q_scale
            mine = lo if p == 0 else jnp.logical_not(lo)
            q_ref[hh, :, :LANES] = (qn * gqn_ref[...] * rinv).astype(BF16)
            q_ref[hh, :, LANES:] = (jnp.where(mine, q_rot, 0.0) * rinv).astype(BF16)
            kn = kvall[:, hs]
            rinvk = lax.rsqrt((jnp.sum(kn * kn, axis=-1, keepdims=True) + ssq_kr) * (1.0 / MLA_QK) + RMS_EPS)
            k_ref[hh, :, :LANES] = (kn * gkn_ref[...] * rinvk).astype(BF16)
            k_ref[hh, :, LANES:] = (k_rot * rinvk).astype(BF16)
            v_ref[hh] = kvall[:, nh + hh * LANES:nh + (hh + 1) * LANES].astype(BF16)


def _attn_kernel(q_ref, k_ref, v_ref, u_ref, pv_ref, o_ref, ub_ref, vt_ref):
    tq = q_ref.shape[1]
    qi = pl.program_id(1)
    ub_ref[...] = u_ref[...].astype(BF16)
    vt_ref[...] = pv_ref[...].T.astype(BF16)
    qs = [q_ref[j] for j in range(ATTN_HEADS)]

    def tile(kt, carry, masked):
        start = pl.multiple_of(kt * TK_ATTN, TK_ATTN)
        if masked:
            q_chunk = (qi * tq + lax.broadcasted_iota(jnp.int32, (tq, 1), 0)) // CHUNK
            k_chunk = (start + lax.broadcasted_iota(jnp.int32, (1, TK_ATTN), 1)) // CHUNK
            allowed = k_chunk <= q_chunk
        out = []
        for j in range(ATTN_HEADS):
            m, l, acc = carry[j]
            kk = k_ref[j, pl.ds(start, TK_ATTN), :]
            vv = v_ref[j, pl.ds(start, TK_ATTN), :]
            s = lax.dot_general(qs[j], kk, (((1,), (1,)), ((), ())), preferred_element_type=F32)
            if masked:
                s = jnp.where(allowed, s, -jnp.inf)
            m_new = jnp.maximum(m, jnp.max(s, axis=-1, keepdims=True))
            alpha = jnp.exp(m - m_new)
            p = jnp.exp(s - m_new)
            l = alpha * l + jnp.sum(p, axis=-1, keepdims=True)
            acc = alpha * acc + jnp.dot(p.astype(BF16), vv, preferred_element_type=F32)
            out.append((m_new, l, acc))
        return tuple(out)

    init = tuple((jnp.full((tq, 1), -jnp.inf, F32), jnp.zeros((tq, 1), F32), jnp.zeros((tq, LANES), F32))
                 for _ in range(ATTN_HEADS))
    n_full = (qi * tq) // TK_ATTN
    carry = lax.fori_loop(0, n_full, lambda kt, c: tile(kt, c, False), init)
    carry = tile(n_full, carry, True)
    for j in range(ATTN_HEADS):
        _, l, acc = carry[j]
        o_ref[:, j * LANES:(j + 1) * LANES] = (acc / l).astype(BF16)


def _topk_ranks(s):
    n = s.shape[0]
    idx = lax.broadcasted_iota(jnp.int32, s.shape, 0)
    rank = jnp.full(s.shape, float(PEER_TOPK), F32)
    vals = []
    for kk in range(PEER_TOPK):
        m = jnp.max(s, axis=0, keepdims=True)
        first = jnp.min(jnp.where(s == m, idx, n), axis=0, keepdims=True)
        sel = idx == first
        rank = jnp.where(sel, float(kk), rank)
        s = jnp.where(sel, -jnp.inf, s)
        vals.append(m)
    return jnp.concatenate(vals, axis=0), rank


def _topk_fast(s):
    rank = jnp.full(s.shape, float(PEER_TOPK), F32)
    vals = []
    for kk in range(PEER_TOPK):
        m = jnp.max(s, axis=0, keepdims=True)
        hit = s == m
        rank = jnp.where(hit, float(kk), rank)
        s = jnp.where(hit, -jnp.inf, s)
        vals.append(m)
    taken = jnp.sum(jnp.where(rank < PEER_TOPK, 1.0, 0.0), axis=0, keepdims=True)
    return jnp.concatenate(vals, axis=0), rank, taken


def _pair_counts(v1, v2):
    t = v1.shape[1]
    row8 = lax.broadcasted_iota(jnp.int32, (8, t), 0)
    cands = [v1[0:1] + v2[0:8], v1[0:1] + v2[8:16]]
    for a in range(1, 8):
        cands.append(jnp.where(row8 < PEER_TOPK // (a + 1), v1[a:a + 1] + v2[0:8], -jnp.inf))
    cands.append(v1[8:16] + v2[0:1])
    work = list(cands)
    cum = jnp.zeros((1, t), F32)
    tau = jnp.zeros((1, t), F32)
    above = jnp.zeros((1, t), F32)
    for _ in range(PEER_TOPK):
        m = jnp.max(functools.reduce(jnp.maximum, work), axis=0, keepdims=True)
        hit = [c == m for c in work]
        cnt = jnp.sum(functools.reduce(jnp.add, [jnp.where(hh, 1.0, 0.0) for hh in hit]), axis=0, keepdims=True)
        active = cum < PEER_TOPK
        tau = jnp.where(active, m, tau)
        above = jnp.where(active, cum, above)
        cum = cum + cnt
        work = [jnp.where(hh, -jnp.inf, c) for hh, c in zip(hit, work)]
    need = PEER_TOPK - above
    rows = [[cands[0], cands[1]]] + [[cands[a + 1]] for a in range(1, 8)]
    n_rows = []
    for a in range(PEER_TOPK):
        cs = rows[a] if a < 8 else [cands[9][a - 8:a - 7]]
        gt = functools.reduce(jnp.add, [jnp.sum(jnp.where(c > tau, 1.0, 0.0), axis=0, keepdims=True) for c in cs])
        eq = functools.reduce(jnp.add, [jnp.sum(jnp.where(c == tau, 1.0, 0.0), axis=0, keepdims=True) for c in cs])
        take = jnp.minimum(eq, jnp.maximum(need, 0.0))
        need = need - eq
        n_rows.append(gt + take)
    return jnp.concatenate(n_rows, axis=0)


def _mid_kernel(x_ref, oa_ref, ob_ref, woa_ref, wob_ref, gffn_ref, wqt_ref, sk_ref,
                x1_ref, xnt_ref, n1_ref, a_ref, r2_ref, b_ref, s_ref, vals_ref, rank_ref):
    tm = x_ref.shape[0]
    x1 = (x_ref[...] + jnp.dot(oa_ref[...], woa_ref[...], preferred_element_type=F32)
          + jnp.dot(ob_ref[...], wob_ref[...], preferred_element_type=F32))
    x1_ref[...] = x1
    xn = x1 * _rms_scale(x1, D_MODEL) * gffn_ref[...]
    xnt = xn.T.astype(BF16)
    xnt_ref[...] = xnt
    qt = jnp.dot(wqt_ref[...], xnt, preferred_element_type=F32).astype(BF16)

    n_hp = 2 * PEER_HEADS
    excess = jnp.zeros((1, tm), F32)
    for hp in range(n_hp):
        sp = jnp.dot(sk_ref[hp], qt[hp * LANES:(hp + 1) * LANES, :], preferred_element_type=F32)
        s_ref[hp] = sp
        vp, rp, taken = _topk_fast(sp)
        vals_ref[hp] = vp
        rank_ref[hp] = rp
        excess = jnp.maximum(excess, taken - PEER_TOPK)

    @pl.when(jnp.max(excess) > 0.0)
    def _():
        for hp in range(n_hp):
            vp, rp = _topk_ranks(s_ref[hp])
            vals_ref[hp] = vp
            rank_ref[hp] = rp

    rowk = lax.broadcasted_iota(jnp.int32, (PEER_TOPK, tm), 0).astype(F32)
    for hh in range(PEER_HEADS):
        v1, v2 = vals_ref[2 * hh], vals_ref[2 * hh + 1]
        n = _pair_counts(v1, v2)
        e1 = jnp.exp(v1 - v1[0:1])
        e2 = jnp.exp(v2 - v2[0:1])
        paired = jnp.zeros((PEER_TOPK, tm), F32)
        for a in range(PEER_TOPK):
            paired = paired + jnp.where(rowk < n[a:a + 1], e1[a:a + 1], 0.0)
        z = jnp.sum(paired * e2, axis=0, keepdims=True)
        rank1 = rank_ref[2 * hh]
        n1 = jnp.zeros((PEER_N_KEYS, tm), F32)
        for a in range(PEER_TOPK):
            n1 = jnp.where(rank1 == float(a), n[a:a + 1], n1)
        n1_ref[hh] = n1
        a_ref[hh] = jnp.exp(s_ref[2 * hh] - v1[0:1]) / z
        r2_ref[hh] = rank_ref[2 * hh + 1].astype(BF16)
        b_ref[hh] = jnp.exp(s_ref[2 * hh + 1] - v2[0:1]).astype(BF16)


def _peer_kernel(n1_ref, a_ref, r2_ref, b_ref, xnt_ref, u_ref, vt_ref, x1_ref, y_ref, acc_ref, g_ref):
    e = pl.program_id(1)
    tt = xnt_ref.shape[1]
    keys = PEER_SUB // PEER_N_KEYS
    pack = 16
    reps = (PEER_N_KEYS // pack, 1)

    @pl.when(e == 0)
    def _():
        acc_ref[...] = jnp.zeros_like(acc_ref)

    def weights(k):
        w = None
        for hh in range(PEER_HEADS):
            n16 = jnp.broadcast_to(n1_ref[hh, k:k + 1, :], (pack, tt)).astype(BF16)
            a16 = jnp.broadcast_to(a_ref[hh, k:k + 1, :], (pack, tt)).astype(BF16)
            term = jnp.where(r2_ref[hh] < jnp.tile(n16, reps), b_ref[hh], jnp.zeros((), BF16)) * jnp.tile(a16, reps)
            w = term if w is None else w + term
        return w

    def activations(c):
        rows = slice(c * PEER_SUB, (c + 1) * PEER_SUB)
        at = jnp.dot(u_ref[rows, :], xnt_ref[...], preferred_element_type=F32)
        g_ref[c % 2] = jax.nn.gelu(at).astype(BF16)

    def accumulate(c):
        rows = slice(c * PEER_SUB, (c + 1) * PEER_SUB)
        wa = jnp.concatenate([g_ref[c % 2, k * PEER_N_KEYS:(k + 1) * PEER_N_KEYS, :] * weights(c * keys + k)
                              for k in range(keys)], axis=0)
        acc_ref[...] += jnp.dot(vt_ref[:, rows], wa, preferred_element_type=F32)

    n_sub = EC_PEER // PEER_SUB
    activations(0)
    for c in range(n_sub):
        if c + 1 < n_sub:
            activations(c + 1)
        accumulate(c)

    @pl.when(e == pl.num_programs(1) - 1)
    def _():
        y_ref[...] = x1_ref[...] + acc_ref[...].T


def _const_spec(shape):
    zeros = (0,) * len(shape)
    return pl.BlockSpec(shape, lambda *_: zeros)


def _params(semantics):
    return pltpu.CompilerParams(dimension_semantics=semantics, vmem_limit_bytes=VMEM_LIMIT)


def kernel(x, norm_mix_g, w_in, gmlp_vnorm_g, gmlp_ws, gmlp_bs, mla_qnorm_g, mla_w_uq, mla_kvnorm_g, mla_w_ukv,
           mla_qk_q_g, mla_qk_k_g, w_out, norm_ffn_g, peer_w_q, peer_subkeys, peer_u, peer_v):
    bsz, seq, d = x.shape
    assert bsz == 1 and d == D_MODEL and seq % TK_ATTN == 0 and TK_ATTN % TQ_ATTN == 0 and seq % TT_PEER == 0
    assert peer_u.shape[1] % ((MLA_HEADS // ATTN_HEADS) * (seq // TQ_ATTN) * LANES) == 0
    x2 = x[0]
    n_exp = peer_u.shape[1]

    rot = (np.arange(MLA_ROPE) + MLA_ROPE // 2) % MLA_ROPE
    c_rope = 2 * GMLP_WIDTH + MLA_Q_RANK + MLA_KV_RANK
    w_rope = w_in[0][:, c_rope:]
    w_in_ext = jnp.concatenate([w_in[0][:, :c_rope], w_rope, w_rope, w_rope[:, rot], w_rope[:, rot]],
                               axis=1).astype(BF16)
    wq3 = mla_w_uq[0].reshape(MLA_Q_RANK, MLA_HEADS, MLA_QK)
    wq_rope = wq3[:, :, MLA_NOPE:]
    w_uq_ext = jnp.concatenate([wq3[:, :, :MLA_NOPE].reshape(MLA_Q_RANK, -1),
                                wq_rope.reshape(MLA_Q_RANK, -1),
                                wq_rope[:, :, rot].reshape(MLA_Q_RANK, -1)], axis=1).astype(BF16)
    wkv3 = mla_w_ukv[0].reshape(MLA_KV_RANK, MLA_HEADS, MLA_NOPE + LANES)
    w_ukv_ext = jnp.concatenate([wkv3[:, :, :MLA_NOPE].reshape(MLA_KV_RANK, -1),
                                 wkv3[:, :, MLA_NOPE:].reshape(MLA_KV_RANK, -1)], axis=1).astype(BF16)
    row = lambda v: v.reshape(1, -1)
    gq, gk = mla_qk_q_g[0], mla_qk_k_g[0]
    dup = lambda v: row(jnp.concatenate([v, v]))
    bsb = jnp.repeat(gmlp_bs[0].T, LANES, axis=1)
    pos = jnp.arange(seq, dtype=F32)
    inv_freq = ROPE_THETA ** (-jnp.arange(0, MLA_ROPE, 2, dtype=F32) / MLA_ROPE)
    ang = pos[:, None] * inv_freq[None, :]
    cos, sin = jnp.cos(ang), jnp.sin(ang)
    cos_f = jnp.concatenate([cos, cos, cos, cos], axis=1)
    sin_f = jnp.concatenate([-sin, sin, -sin, sin], axis=1)

    n_front = seq // TM_FRONT
    rows = lambda w: pl.BlockSpec((TM_FRONT, w), lambda i: (i, 0))
    heads = lambda w: pl.BlockSpec((MLA_HEADS, TM_FRONT, w), lambda i: (0, i, 0))
    oa, q, k, v = pl.pallas_call(
        _front_kernel,
        grid=(n_front,),
        in_specs=[rows(D_MODEL), _const_spec((1, D_MODEL)), _const_spec(w_in_ext.shape),
                  _const_spec((1, GMLP_WIDTH)), _const_spec(gmlp_ws[0].shape), _const_spec(bsb.shape),
                  _const_spec((1, MLA_Q_RANK)), _const_spec(w_uq_ext.shape),
                  _const_spec((1, MLA_KV_RANK)), _const_spec(w_ukv_ext.shape),
                  _const_spec((1, LANES)), _const_spec((1, LANES)), _const_spec((1, LANES)),
                  _const_spec((1, LANES)), _const_spec((1, LANES)), _const_spec((1, LANES)),
                  rows(LANES), rows(LANES)],
        out_specs=[rows(GMLP_WIDTH), heads(QK_PAD), heads(QK_PAD), heads(LANES)],
        out_shape=[jax.ShapeDtypeStruct((seq, GMLP_WIDTH), BF16),
                   jax.ShapeDtypeStruct((MLA_HEADS, seq, QK_PAD), BF16),
                   jax.ShapeDtypeStruct((MLA_HEADS, seq, QK_PAD), BF16),
                   jax.ShapeDtypeStruct((MLA_HEADS, seq, LANES), BF16)],
        compiler_params=_params(("parallel",)),
        name="front",
    )(x2, row(norm_mix_g[0]), w_in_ext, row(gmlp_vnorm_g[0]), gmlp_ws[0], bsb,
      row(mla_qnorm_g[0]), w_uq_ext, row(mla_kvnorm_g[0]), w_ukv_ext,
      row(gq[:MLA_NOPE]), dup(gq[MLA_NOPE:]), dup(gq[MLA_NOPE:][rot]),
      row(gk[:MLA_NOPE]), dup(gk[MLA_NOPE:]), dup(gk[MLA_NOPE:][rot]), cos_f, sin_f)

    n_q = seq // TQ_ATTN
    n_attn_steps = (MLA_HEADS // ATTN_HEADS) * n_q
    slab = n_exp // n_attn_steps
    ob, u_bf, vt_bf = pl.pallas_call(
        _attn_kernel,
        grid=(MLA_HEADS // ATTN_HEADS, n_q),
        in_specs=[pl.BlockSpec((ATTN_HEADS, TQ_ATTN, QK_PAD), lambda h, i: (h, i, 0)),
                  pl.BlockSpec((ATTN_HEADS, seq, QK_PAD), lambda h, i: (h, 0, 0)),
                  pl.BlockSpec((ATTN_HEADS, seq, LANES), lambda h, i: (h, 0, 0)),
                  pl.BlockSpec((slab, D_MODEL), lambda h, i: (h * n_q + i, 0)),
                  pl.BlockSpec((slab, D_MODEL), lambda h, i: (h * n_q + i, 0))],
        out_specs=[pl.BlockSpec((TQ_ATTN, ATTN_HEADS * LANES), lambda h, i: (i, h)),
                   pl.BlockSpec((slab, D_MODEL), lambda h, i: (h * n_q + i, 0)),
                   pl.BlockSpec((D_MODEL, slab), lambda h, i: (0, h * n_q + i))],
        out_shape=[jax.ShapeDtypeStruct((seq, MLA_HEADS * LANES), BF16),
                   jax.ShapeDtypeStruct((n_exp, D_MODEL), BF16),
                   jax.ShapeDtypeStruct((D_MODEL, n_exp), BF16)],
        compiler_params=_params(("parallel", "parallel")),
        name="attn",
    )(q, k, v, peer_u[0], peer_v[0])

    wo = w_out[0].astype(BF16)
    wqt = peer_w_q[0].T.astype(BF16)
    sk = peer_subkeys[0].reshape(PEER_HEADS * 2, PEER_N_KEYS, -1).astype(BF16)
    n_mid = seq // TM_MID
    mrows = lambda w: pl.BlockSpec((TM_MID, w), lambda i: (i, 0))
    tok = pl.BlockSpec((PEER_HEADS, PEER_N_KEYS, TM_MID), lambda i: (0, 0, i))
    tok_shape = lambda dt: jax.ShapeDtypeStruct((PEER_HEADS, PEER_N_KEYS, seq), dt)
    x1, xnt, n1, a1, r2, b2 = pl.pallas_call(
        _mid_kernel,
        grid=(n_mid,),
        in_specs=[mrows(D_MODEL), mrows(GMLP_WIDTH), mrows(MLA_HEADS * LANES),
                  _const_spec((GMLP_WIDTH, D_MODEL)), _const_spec((MLA_HEADS * LANES, D_MODEL)),
                  _const_spec((1, D_MODEL)), _const_spec(wqt.shape), _const_spec(sk.shape)],
        out_specs=[mrows(D_MODEL), pl.BlockSpec((D_MODEL, TM_MID), lambda i: (0, i)), tok, tok, tok, tok],
        out_shape=[jax.ShapeDtypeStruct((seq, D_MODEL), F32), jax.ShapeDtypeStruct((D_MODEL, seq), BF16),
                   tok_shape(F32), tok_shape(F32), tok_shape(BF16), tok_shape(BF16)],
        scratch_shapes=[pltpu.VMEM((2 * PEER_HEADS, PEER_N_KEYS, TM_MID), F32),
                        pltpu.VMEM((2 * PEER_HEADS, PEER_TOPK, TM_MID), F32),
                        pltpu.VMEM((2 * PEER_HEADS, PEER_N_KEYS, TM_MID), F32)],
        compiler_params=_params(("parallel",)),
        name="mid",
    )(x2, oa, ob, wo[:GMLP_WIDTH], wo[GMLP_WIDTH:], row(norm_ffn_g[0]), wqt, sk)

    keys_per_step = EC_PEER // PEER_N_KEYS
    first_half = pl.BlockSpec((PEER_HEADS, keys_per_step, TT_PEER), lambda t, e: (0, e, t))
    second_half = pl.BlockSpec((PEER_HEADS, PEER_N_KEYS, TT_PEER), lambda t, e: (0, 0, t))
    y = pl.pallas_call(
        _peer_kernel,
        grid=(seq // TT_PEER, n_exp // EC_PEER),
        in_specs=[first_half, first_half, second_half, second_half,
                  pl.BlockSpec((D_MODEL, TT_PEER), lambda t, e: (0, t)),
                  pl.BlockSpec((EC_PEER, D_MODEL), lambda t, e: (e, 0)),
                  pl.BlockSpec((D_MODEL, EC_PEER), lambda t, e: (0, e)),
                  pl.BlockSpec((TT_PEER, D_MODEL), lambda t, e: (t, 0))],
        out_specs=pl.BlockSpec((TT_PEER, D_MODEL), lambda t, e: (t, 0)),
        out_shape=jax.ShapeDtypeStruct((seq, D_MODEL), F32),
        scratch_shapes=[pltpu.VMEM((D_MODEL, TT_PEER), F32), pltpu.VMEM((2, PEER_SUB, TT_PEER), BF16)],
        compiler_params=_params(("parallel", "arbitrary")),
        name="peer",
    )(n1, a1, r2, b2, xnt, u_bf, vt_bf, x1)
    return y[None]
```

```python
import functools

import jax
import jax.numpy as jnp
import numpy as np
from jax import lax
from jax.experimental import pallas as pl
from jax.experimental.pallas import tpu as pltpu

F32 = jnp.float32
BF16 = jnp.bfloat16

D_MODEL = 2048
CHUNK = 64
GMLP_HEADS = 8
GMLP_WIDTH = 1024
GMLP_BLOCK = 128
MLA_HEADS = 8
MLA_NOPE = 128
MLA_ROPE = 64
MLA_QK = MLA_NOPE + MLA_ROPE
MLA_Q_RANK = 512
MLA_KV_RANK = 256
ROPE_THETA = 10000.0
PEER_HEADS = 8
PEER_N_KEYS = 128
PEER_TOPK = 16
RMS_EPS = 1e-6
LANES = 128
QK_PAD = 2 * LANES

TM_FRONT = 256
TQ_ATTN = 512
TK_ATTN = 1024
ATTN_HEADS = 2
TM_MID = 256
TT_PEER = 512
EC_PEER = 1024
PEER_SUB = 512
VMEM_LIMIT = 56 * 1024 * 1024


def _rms_scale(v, width):
    return lax.rsqrt(jnp.sum(v * v, axis=-1, keepdims=True) * (1.0 / width) + RMS_EPS)


def _front_kernel(x_ref, gmix_ref, win_ref, wrope_ref, gv_ref, ws_ref, bsb_ref, gq_ref, wuq_ref, gkv_ref, wukv_ref,
                  gqn_ref, gqr_ref, gqrr_ref, gkn_ref, gkr_ref, gkrr_ref,
                  cos_t_ref, sin_t_ref, sins_t_ref, cos_o_ref, sin_o_ref, sins_o_ref,
                  oa_ref, q_ref, k_ref, v_ref):
    tm = x_ref.shape[0]
    x = x_ref[...]
    h = (x * _rms_scale(x, D_MODEL) * gmix_ref[...]).astype(BF16)
    c_rope = 2 * GMLP_WIDTH + MLA_Q_RANK + MLA_KV_RANK
    proj = jnp.dot(h, win_ref[:, :c_rope], preferred_element_type=F32)
    rope = jnp.dot(h, wrope_ref[...], preferred_element_type=F32)

    zu = jax.nn.gelu(proj[:, :GMLP_WIDTH])
    zv = jax.nn.gelu(proj[:, GMLP_WIDTH:2 * GMLP_WIDTH])
    vn = (zv * _rms_scale(zv, GMLP_WIDTH) * gv_ref[...]).astype(BF16)
    row = lax.broadcasted_iota(jnp.int32, (GMLP_BLOCK, GMLP_BLOCK), 0)
    col = lax.broadcasted_iota(jnp.int32, (GMLP_BLOCK, GMLP_BLOCK), 1)
    tri = col <= row
    for hh in range(GMLP_HEADS):
        w = jnp.where(tri, ws_ref[hh], 0.0).astype(BF16)
        cs = slice(hh * LANES, (hh + 1) * LANES)
        for r in range(tm // GMLP_BLOCK):
            rs = slice(r * GMLP_BLOCK, (r + 1) * GMLP_BLOCK)
            sv = jnp.dot(w, vn[rs, cs], preferred_element_type=F32) + bsb_ref[:, cs]
            oa_ref[rs, cs] = (zu[rs, cs] * sv).astype(BF16)

    c0 = 2 * GMLP_WIDTH
    cq = proj[:, c0:c0 + MLA_Q_RANK]
    ckv = proj[:, c0 + MLA_Q_RANK:c0 + MLA_Q_RANK + MLA_KV_RANK]
    kr = rope[:, :LANES]
    krr = rope[:, LANES:]
    cqn = (cq * _rms_scale(cq, MLA_Q_RANK) * gq_ref[...]).astype(BF16)
    qall = jnp.dot(cqn, wuq_ref[...], preferred_element_type=F32)
    ckvn = (ckv * _rms_scale(ckv, MLA_KV_RANK) * gkv_ref[...]).astype(BF16)
    kvall = jnp.dot(ckvn, wukv_ref[...], preferred_element_type=F32)

    cos_f = cos_t_ref[...] * cos_o_ref[...] - sin_t_ref[...] * sin_o_ref[...]
    sin_f = sins_t_ref[...] * cos_o_ref[...] + cos_t_ref[...] * sins_o_ref[...]
    lane = lax.broadcasted_iota(jnp.int32, (tm, LANES), 1)
    lo = lane < MLA_ROPE
    k_rot = kr * gkr_ref[...] * cos_f + krr * gkrr_ref[...] * sin_f
    ssq_kr = jnp.sum(jnp.where(lo, kr * kr, 0.0), axis=-1, keepdims=True)
    q_scale = MLA_QK ** -0.5
    nh = MLA_HEADS * MLA_NOPE
    for g in range(MLA_HEADS // 2):
        qr = qall[:, nh + g * LANES:nh + (g + 1) * LANES]
        qrr = qall[:, nh + 512 + g * LANES:nh + 512 + (g + 1) * LANES]
        q_rot = qr * gqr_ref[...] * cos_f + qrr * gqrr_ref[...] * sin_f
        qr2 = qr * qr
        ssq_half = (jnp.sum(jnp.where(lo, qr2, 0.0), axis=-1, keepdims=True),
                    jnp.sum(jnp.where(lo, 0.0, qr2), axis=-1, keepdims=True))
        for p in range(2):
            hh = 2 * g + p
            hs = slice(hh * LANES, (hh + 1) * LANES)
            qn = qall[:, hs]
            rinv = lax.rsqrt((jnp.sum(qn * qn, axis=-1, keepdims=True) + ssq_half[p]) * (1.0 / MLA_QK)
                             + RMS_EPS) * q_scale
            mine = lo if p == 0 else jnp.logical_not(lo)
            q_ref[hh, :, :LANES] = (qn * gqn_ref[...] * rinv).astype(BF16)
            q_ref[hh, :, LANES:] = (jnp.where(mine, q_rot, 0.0) * rinv).astype(BF16)
            kn = kvall[:, hs]
            rinvk = lax.rsqrt((jnp.sum(kn * kn, axis=-1, keepdims=True) + ssq_kr) * (1.0 / MLA_QK) + RMS_EPS)
            k_ref[hh, :, :LANES] = (kn * gkn_ref[...] * rinvk).astype(BF16)
            k_ref[hh, :, LANES:] = (k_rot * rinvk).astype(BF16)
            v_ref[hh] = kvall[:, nh + hh * LANES:nh + (hh + 1) * LANES].astype(BF16)


def _attn_kernel(q_ref, k_ref, v_ref, u_ref, pv_ref, o_ref, ub_ref, vt_ref):
    tq = q_ref.shape[1]
    qi = pl.program_id(1)
    ub_ref[...] = u_ref[...].astype(BF16)
    vt_ref[...] = pv_ref[...].T.astype(BF16)
    qs = [q_ref[j] for j in range(ATTN_HEADS)]

    def tile(kt, carry, masked):
        start = pl.multiple_of(kt * TK_ATTN, TK_ATTN)
        if masked:
            q_chunk = (qi * tq + lax.broadcasted_iota(jnp.int32, (tq, 1), 0)) // CHUNK
            k_chunk = (start + lax.broadcasted_iota(jnp.int32, (1, TK_ATTN), 1)) // CHUNK
            allowed = k_chunk <= q_chunk
        out = []
        for j in range(ATTN_HEADS):
            m, l, acc = carry[j]
            kk = k_ref[j, pl.ds(start, TK_ATTN), :]
            vv = v_ref[j, pl.ds(start, TK_ATTN), :]
            s = lax.dot_general(qs[j], kk, (((1,), (1,)), ((), ())), preferred_element_type=F32)
            if masked:
                s = jnp.where(allowed, s, -jnp.inf)
            m_new = jnp.maximum(m, jnp.max(s, axis=-1, keepdims=True))
            alpha = jnp.exp(m - m_new)
            p = jnp.exp(s - m_new)
            l = alpha * l + jnp.sum(p, axis=-1, keepdims=True)
            acc = alpha * acc + jnp.dot(p.astype(BF16), vv, preferred_element_type=F32)
            out.append((m_new, l, acc))
        return tuple(out)

    init = tuple((jnp.full((tq, 1), -jnp.inf, F32), jnp.zeros((tq, 1), F32), jnp.zeros((tq, LANES), F32))
                 for _ in range(ATTN_HEADS))
    n_full = (qi * tq) // TK_ATTN
    carry = lax.fori_loop(0, n_full, lambda kt, c: tile(kt, c, False), init)
    carry = tile(n_full, carry, True)
    for j in range(ATTN_HEADS):
        _, l, acc = carry[j]
        o_ref[:, j * LANES:(j + 1) * LANES] = (acc / l).astype(BF16)


def _topk_ranks(s):
    n = s.shape[0]
    idx = lax.broadcasted_iota(jnp.int32, s.shape, 0)
    rank = jnp.full(s.shape, float(PEER_TOPK), F32)
    vals = []
    for kk in range(PEER_TOPK):
        m = jnp.max(s, axis=0, keepdims=True)
        first = jnp.min(jnp.where(s == m, idx, n), axis=0, keepdims=True)
        sel = idx == first
        rank = jnp.where(sel, float(kk), rank)
        s = jnp.where(sel, -jnp.inf, s)
        vals.append(m)
    return jnp.concatenate(vals, axis=0), rank


def _topk_fast(s):
    rank = jnp.full(s.shape, float(PEER_TOPK), F32)
    vals = []
    for kk in range(PEER_TOPK):
        m = jnp.max(s, axis=0, keepdims=True)
        hit = s == m
        rank = jnp.where(hit, float(kk), rank)
        s = jnp.where(hit, -jnp.inf, s)
        vals.append(m)
    taken = jnp.sum(jnp.where(rank < PEER_TOPK, 1.0, 0.0), axis=0, keepdims=True)
    return jnp.concatenate(vals, axis=0), rank, taken


def _pair_counts(v1, v2):
    t = v1.shape[1]
    row8 = lax.broadcasted_iota(jnp.int32, (8, t), 0)
    cands = [v1[0:1] + v2[0:8], v1[0:1] + v2[8:16]]
    for a in range(1, 8):
        cands.append(jnp.where(row8 < PEER_TOPK // (a + 1), v1[a:a + 1] + v2[0:8], -jnp.inf))
    cands.append(v1[8:16] + v2[0:1])
    work = list(cands)
    cum = jnp.zeros((1, t), F32)
    tau = jnp.zeros((1, t), F32)
    above = jnp.zeros((1, t), F32)
    for _ in range(PEER_TOPK):
        m = jnp.max(functools.reduce(jnp.maximum, work), axis=0, keepdims=True)
        hit = [c == m for c in work]
        cnt = jnp.sum(functools.reduce(jnp.add, [jnp.where(hh, 1.0, 0.0) for hh in hit]), axis=0, keepdims=True)
        active = cum < PEER_TOPK
        tau = jnp.where(active, m, tau)
        above = jnp.where(active, cum, above)
        cum = cum + cnt
        work = [jnp.where(hh, -jnp.inf, c) for hh, c in zip(hit, work)]
    need = PEER_TOPK - above
    rows = [[cands[0], cands[1]]] + [[cands[a + 1]] for a in range(1, 8)]
    n_rows = []
    for a in range(PEER_TOPK):
        cs = rows[a] if a < 8 else [cands[9][a - 8:a - 7]]
        gt = functools.reduce(jnp.add, [jnp.sum(jnp.where(c > tau, 1.0, 0.0), axis=0, keepdims=True) for c in cs])
        eq = functools.reduce(jnp.add, [jnp.sum(jnp.where(c == tau, 1.0, 0.0), axis=0, keepdims=True) for c in cs])
        take = jnp.minimum(eq, jnp.maximum(need, 0.0))
        need = need - eq
        n_rows.append(gt + take)
    return jnp.concatenate(n_rows, axis=0)


def _mid_kernel(x_ref, oa_ref, ob_ref, woa_ref, wob_ref, gffn_ref, wqt_ref, sk_ref,
                x1_ref, xnt_ref, n1_ref, a_ref, r2_ref, b_ref, s_ref, vals_ref, rank_ref):
    tm = x_ref.shape[0]
    x1 = (x_ref[...] + jnp.dot(oa_ref[...], woa_ref[...], preferred_element_type=F32)
          + jnp.dot(ob_ref[...], wob_ref[...], preferred_element_type=F32))
    x1_ref[...] = x1
    xn = x1 * _rms_scale(x1, D_MODEL) * gffn_ref[...]
    xnt = xn.T.astype(BF16)
    xnt_ref[...] = xnt
    qt = jnp.dot(wqt_ref[...], xnt, preferred_element_type=F32).astype(BF16)

    n_hp = 2 * PEER_HEADS
    excess = jnp.zeros((1, tm), F32)
    for hp in range(n_hp):
        sp = jnp.dot(sk_ref[hp], qt[hp * LANES:(hp + 1) * LANES, :], preferred_element_type=F32)
        s_ref[hp] = sp
        vp, rp, taken = _topk_fast(sp)
        vals_ref[hp] = vp
        rank_ref[hp] = rp
        excess = jnp.maximum(excess, taken - PEER_TOPK)

    @pl.when(jnp.max(excess) > 0.0)
    def _():
        for hp in range(n_hp):
            vp, rp = _topk_ranks(s_ref[hp])
            vals_ref[hp] = vp
            rank_ref[hp] = rp

    rowk = lax.broadcasted_iota(jnp.int32, (PEER_TOPK, tm), 0).astype(F32)
    for hh in range(PEER_HEADS):
        v1, v2 = vals_ref[2 * hh], vals_ref[2 * hh + 1]
        n = _pair_counts(v1, v2)
        e1 = jnp.exp(v1 - v1[0:1])
        e2 = jnp.exp(v2 - v2[0:1])
        paired = jnp.zeros((PEER_TOPK, tm), F32)
        for a in range(PEER_TOPK):
            paired = paired + jnp.where(rowk < n[a:a + 1], e1[a:a + 1], 0.0)
        z = jnp.sum(paired * e2, axis=0, keepdims=True)
        rank1 = rank_ref[2 * hh]
        n1 = jnp.zeros((PEER_N_KEYS, tm), F32)
        for a in range(PEER_TOPK):
            n1 = jnp.where(rank1 == float(a), n[a:a + 1], n1)
        n1_ref[hh] = n1
        a_ref[hh] = jnp.exp(s_ref[2 * hh] - v1[0:1]) / z
        r2_ref[hh] = rank_ref[2 * hh + 1].astype(BF16)
        b_ref[hh] = jnp.exp(s_ref[2 * hh + 1] - v2[0:1]).astype(BF16)


def _peer_kernel(n1_ref, a_ref, r2_ref, b_ref, xnt_ref, u_ref, vt_ref, x1_ref, y_ref, acc_ref, g_ref):
    e = pl.program_id(1)
    tt = xnt_ref.shape[1]
    keys = PEER_SUB // PEER_N_KEYS
    pack = 16
    reps = (PEER_N_KEYS // pack, 1)

    @pl.when(e == 0)
    def _():
        acc_ref[...] = jnp.zeros_like(acc_ref)

    def weights(k):
        w = None
        for hh in range(PEER_HEADS):
            n16 = jnp.broadcast_to(n1_ref[hh, k:k + 1, :], (pack, tt)).astype(BF16)
            a16 = jnp.broadcast_to(a_ref[hh, k:k + 1, :], (pack, tt)).astype(BF16)
            term = jnp.where(r2_ref[hh] < jnp.tile(n16, reps), b_ref[hh], jnp.zeros((), BF16)) * jnp.tile(a16, reps)
            w = term if w is None else w + term
        return w

    def activations(c):
        rows = slice(c * PEER_SUB, (c + 1) * PEER_SUB)
        at = jnp.dot(u_ref[rows, :], xnt_ref[...], preferred_element_type=F32)
        g_ref[c % 2] = jax.nn.gelu(at).astype(BF16)

    def accumulate(c):
        rows = slice(c * PEER_SUB, (c + 1) * PEER_SUB)
        wa = jnp.concatenate([g_ref[c % 2, k * PEER_N_KEYS:(k + 1) * PEER_N_KEYS, :] * weights(c * keys + k)
                              for k in range(keys)], axis=0)
        acc_ref[...] += jnp.dot(vt_ref[:, rows], wa, preferred_element_type=F32)

    n_sub = EC_PEER // PEER_SUB
    activations(0)
    for c in range(n_sub):
        if c + 1 < n_sub:
            activations(c + 1)
        accumulate(c)

    @pl.when(e == pl.num_programs(1) - 1)
    def _():
        y_ref[...] = x1_ref[...] + acc_ref[...].T


def _const_spec(shape):
    zeros = (0,) * len(shape)
    return pl.BlockSpec(shape, lambda *_: zeros)


def _params(semantics):
    return pltpu.CompilerParams(dimension_semantics=semantics, vmem_limit_bytes=VMEM_LIMIT)


def kernel(x, norm_mix_g, w_in, gmlp_vnorm_g, gmlp_ws, gmlp_bs, mla_qnorm_g, mla_w_uq, mla_kvnorm_g, mla_w_ukv,
           mla_qk_q_g, mla_qk_k_g, w_out, norm_ffn_g, peer_w_q, peer_subkeys, peer_u, peer_v):
    bsz, seq, d = x.shape
    assert bsz == 1 and d == D_MODEL and seq % TK_ATTN == 0 and TK_ATTN % TQ_ATTN == 0 and seq % TT_PEER == 0
    assert peer_u.shape[1] % ((MLA_HEADS // ATTN_HEADS) * (seq // TQ_ATTN) * LANES) == 0
    x2 = x[0]
    n_exp = peer_u.shape[1]

    rot = (np.arange(MLA_ROPE) + MLA_ROPE // 2) % MLA_ROPE
    c_rope = 2 * GMLP_WIDTH + MLA_Q_RANK + MLA_KV_RANK
    w_in_bf = w_in[0].astype(BF16)
    w_rope = w_in_bf[:, c_rope:]
    w_rope_ext = jnp.concatenate([w_rope, w_rope, w_rope[:, rot], w_rope[:, rot]], axis=1)
    wq3 = mla_w_uq[0].reshape(MLA_Q_RANK, MLA_HEADS, MLA_QK)
    wq_rope = wq3[:, :, MLA_NOPE:]
    w_uq_ext = jnp.concatenate([wq3[:, :, :MLA_NOPE].reshape(MLA_Q_RANK, -1),
                                wq_rope.reshape(MLA_Q_RANK, -1),
                                wq_rope[:, :, rot].reshape(MLA_Q_RANK, -1)], axis=1).astype(BF16)
    wkv3 = mla_w_ukv[0].reshape(MLA_KV_RANK, MLA_HEADS, MLA_NOPE + LANES)
    w_ukv_ext = jnp.concatenate([wkv3[:, :, :MLA_NOPE].reshape(MLA_KV_RANK, -1),
                                 wkv3[:, :, MLA_NOPE:].reshape(MLA_KV_RANK, -1)], axis=1).astype(BF16)
    row = lambda v: v.reshape(1, -1)
    gq, gk = mla_qk_q_g[0], mla_qk_k_g[0]
    dup = lambda v: row(jnp.concatenate([v, v]))
    bsb = jnp.repeat(gmlp_bs[0].T, LANES, axis=1)
    inv_freq = ROPE_THETA ** (-jnp.arange(0, MLA_ROPE, 2, dtype=F32) / MLA_ROPE)
    n_front = seq // TM_FRONT
    lanes4 = lambda t: jnp.concatenate([t, t, t, t], axis=1)
    signed = lambda t: jnp.concatenate([-t, t, -t, t], axis=1)
    ang_t = (jnp.arange(n_front, dtype=F32) * TM_FRONT)[:, None] * inv_freq[None, :]
    ang_o = jnp.arange(TM_FRONT, dtype=F32)[:, None] * inv_freq[None, :]
    tile3 = lambda t: t.reshape(n_front, 1, LANES)
    cos_t, sin_t, sins_t = tile3(lanes4(jnp.cos(ang_t))), tile3(lanes4(jnp.sin(ang_t))), tile3(signed(jnp.sin(ang_t)))
    cos_o, sin_o, sins_o = lanes4(jnp.cos(ang_o)), lanes4(jnp.sin(ang_o)), signed(jnp.sin(ang_o))

    rows = lambda w: pl.BlockSpec((TM_FRONT, w), lambda i: (i, 0))
    per_tile = pl.BlockSpec((None, 1, LANES), lambda i: (i, 0, 0))
    heads = lambda w: pl.BlockSpec((MLA_HEADS, TM_FRONT, w), lambda i: (0, i, 0))
    oa, q, k, v = pl.pallas_call(
        _front_kernel,
        grid=(n_front,),
        in_specs=[rows(D_MODEL), _const_spec((1, D_MODEL)), _const_spec(w_in_bf.shape), _const_spec(w_rope_ext.shape),
                  _const_spec((1, GMLP_WIDTH)), _const_spec(gmlp_ws[0].shape), _const_spec(bsb.shape),
                  _const_spec((1, MLA_Q_RANK)), _const_spec(w_uq_ext.shape),
                  _const_spec((1, MLA_KV_RANK)), _const_spec(w_ukv_ext.shape),
                  _const_spec((1, LANES)), _const_spec((1, LANES)), _const_spec((1, LANES)),
                  _const_spec((1, LANES)), _const_spec((1, LANES)), _const_spec((1, LANES)),
                  per_tile, per_tile, per_tile,
                  _const_spec((TM_FRONT, LANES)), _const_spec((TM_FRONT, LANES)), _const_spec((TM_FRONT, LANES))],
        out_specs=[rows(GMLP_WIDTH), heads(QK_PAD), heads(QK_PAD), heads(LANES)],
        out_shape=[jax.ShapeDtypeStruct((seq, GMLP_WIDTH), BF16),
                   jax.ShapeDtypeStruct((MLA_HEADS, seq, QK_PAD), BF16),
                   jax.ShapeDtypeStruct((MLA_HEADS, seq, QK_PAD), BF16),
                   jax.ShapeDtypeStruct((MLA_HEADS, seq, LANES), BF16)],
        compiler_params=_params(("parallel",)),
        name="front",
    )(x2, row(norm_mix_g[0]), w_in_bf, w_rope_ext, row(gmlp_vnorm_g[0]), gmlp_ws[0], bsb,
      row(mla_qnorm_g[0]), w_uq_ext, row(mla_kvnorm_g[0]), w_ukv_ext,
      row(gq[:MLA_NOPE]), dup(gq[MLA_NOPE:]), dup(gq[MLA_NOPE:][rot]),
      row(gk[:MLA_NOPE]), dup(gk[MLA_NOPE:]), dup(gk[MLA_NOPE:][rot]),
      cos_t, sin_t, sins_t, cos_o, sin_o, sins_o)

    n_q = seq // TQ_ATTN
    n_attn_steps = (MLA_HEADS // ATTN_HEADS) * n_q
    slab = n_exp // n_attn_steps
    ob, u_bf, vt_bf = pl.pallas_call(
        _attn_kernel,
        grid=(MLA_HEADS // ATTN_HEADS, n_q),
        in_specs=[pl.BlockSpec((ATTN_HEADS, TQ_ATTN, QK_PAD), lambda h, i: (h, i, 0)),
                  pl.BlockSpec((ATTN_HEADS, seq, QK_PAD), lambda h, i: (h, 0, 0)),
                  pl.BlockSpec((ATTN_HEADS, seq, LANES), lambda h, i: (h, 0, 0)),
                  pl.BlockSpec((slab, D_MODEL), lambda h, i: (h * n_q + i, 0)),
                  pl.BlockSpec((slab, D_MODEL), lambda h, i: (h * n_q + i, 0))],
        out_specs=[pl.BlockSpec((TQ_ATTN, ATTN_HEADS * LANES), lambda h, i: (i, h)),
                   pl.BlockSpec((slab, D_MODEL), lambda h, i: (h * n_q + i, 0)),
                   pl.BlockSpec((D_MODEL, slab), lambda h, i: (0, h * n_q + i))],
        out_shape=[jax.ShapeDtypeStruct((seq, MLA_HEADS * LANES), BF16),
                   jax.ShapeDtypeStruct((n_exp, D_MODEL), BF16),
                   jax.ShapeDtypeStruct((D_MODEL, n_exp), BF16)],
        compiler_params=_params(("parallel", "parallel")),
        name="attn",
    )(q, k, v, peer_u[0], peer_v[0])

    wo = w_out[0].astype(BF16)
    wqt = peer_w_q[0].T.astype(BF16)
    sk = peer_subkeys[0].reshape(PEER_HEADS * 2, PEER_N_KEYS, -1).astype(BF16)
    n_mid = seq // TM_MID
    mrows = lambda w: pl.BlockSpec((TM_MID, w), lambda i: (i, 0))
    tok = pl.BlockSpec((PEER_HEADS, PEER_N_KEYS, TM_MID), lambda i: (0, 0, i))
    tok_shape = lambda dt: jax.ShapeDtypeStruct((PEER_HEADS, PEER_N_KEYS, seq), dt)
    x1, xnt, n1, a1, r2, b2 = pl.pallas_call(
        _mid_kernel,
        grid=(n_mid,),
        in_specs=[mrows(D_MODEL), mrows(GMLP_WIDTH), mrows(MLA_HEADS * LANES),
                  _const_spec((GMLP_WIDTH, D_MODEL)), _const_spec((MLA_HEADS * LANES, D_MODEL)),
                  _const_spec((1, D_MODEL)), _const_spec(wqt.shape), _const_spec(sk.shape)],
        out_specs=[mrows(D_MODEL), pl.BlockSpec((D_MODEL, TM_MID), lambda i: (0, i)), tok, tok, tok, tok],
        out_shape=[jax.ShapeDtypeStruct((seq, D_MODEL), F32), jax.ShapeDtypeStruct((D_MODEL, seq), BF16),
                   tok_shape(F32), tok_shape(F32), tok_shape(BF16), tok_shape(BF16)],
        scratch_shapes=[pltpu.VMEM((2 * PEER_HEADS, PEER_N_KEYS, TM_MID), F32),
                        pltpu.VMEM((2 * PEER_HEADS, PEER_TOPK, TM_MID), F32),
                        pltpu.VMEM((2 * PEER_HEADS, PEER_N_KEYS, TM_MID), F32)],
        compiler_params=_params(("parallel",)),
        name="mid",
    )(x2, oa, ob, wo[:GMLP_WIDTH], wo[GMLP_WIDTH:], row(norm_ffn_g[0]), wqt, sk)

    keys_per_step = EC_PEER // PEER_N_KEYS
    first_half = pl.BlockSpec((PEER_HEADS, keys_per_step, TT_PEER), lambda t, e: (0, e, t))
    second_half = pl.BlockSpec((PEER_HEADS, PEER_N_KEYS, TT_PEER), lambda t, e: (0, 0, t))
    y = pl.pallas_call(
        _peer_kernel,
        grid=(seq // TT_PEER, n_exp // EC_PEER),
        in_specs=[first_half, first_half, second_half, second_half,
                  pl.BlockSpec((D_MODEL, TT_PEER), lambda t, e: (0, t)),
                  pl.BlockSpec((EC_PEER, D_MODEL), lambda t, e: (e, 0)),
                  pl.BlockSpec((D_MODEL, EC_PEER), lambda t, e: (0, e)),
                  pl.BlockSpec((TT_PEER, D_MODEL), lambda t, e: (t, 0))],
        out_specs=pl.BlockSpec((TT_PEER, D_MODEL), lambda t, e: (t, 0)),
        out_shape=jax.ShapeDtypeStruct((seq, D_MODEL), F32),
        scratch_shapes=[pltpu.VMEM((D_MODEL, TT_PEER), F32), pltpu.VMEM((2, PEER_SUB, TT_PEER), BF16)],
        compiler_params=_params(("parallel", "arbitrary")),
        name="peer",
    )(n1, a1, r2, b2, xnt, u_bf, vt_bf, x1)
    return y[None]
```

```python
import functools

import jax
import jax.numpy as jnp
import numpy as np
from jax import lax
from jax.experimental import pallas as pl
from jax.experimental.pallas import tpu as pltpu

F32 = jnp.float32
BF16 = jnp.bfloat16

D_MODEL = 2048
CHUNK = 64
GMLP_HEADS = 8
GMLP_WIDTH = 1024
GMLP_BLOCK = 128
MLA_HEADS = 8
MLA_NOPE = 128
MLA_ROPE = 64
MLA_QK = MLA_NOPE + MLA_ROPE
MLA_Q_RANK = 512
MLA_KV_RANK = 256
ROPE_THETA = 10000.0
PEER_HEADS = 8
PEER_N_KEYS = 128
PEER_TOPK = 16
RMS_EPS = 1e-6
LANES = 128
QK_PAD = 2 * LANES

TM_FRONT = 512
TQ_ATTN = 512
TK_ATTN = 1024
ATTN_HEADS = 2
TM_MID = 256
TT_PEER = 512
EC_PEER = 1024
PEER_SUB = 512
VMEM_LIMIT = 56 * 1024 * 1024


def _rms_scale(v, width):
    return lax.rsqrt(jnp.sum(v * v, axis=-1, keepdims=True) * (1.0 / width) + RMS_EPS)


def _front_kernel(x_ref, gmix_ref, win_ref, wrope_ref, gv_ref, ws_ref, bsb_ref, gq_ref, wuq_ref, gkv_ref, wukv_ref,
                  gqn_ref, gqr_ref, gqrr_ref, gkn_ref, gkr_ref, gkrr_ref,
                  cos_t_ref, sin_t_ref, sins_t_ref, cos_o_ref, sin_o_ref, sins_o_ref,
                  oa_ref, q_ref, k_ref, v_ref):
    tm = x_ref.shape[0]
    x = x_ref[...]
    h = (x * _rms_scale(x, D_MODEL) * gmix_ref[...]).astype(BF16)
    c_rope = 2 * GMLP_WIDTH + MLA_Q_RANK + MLA_KV_RANK
    proj = jnp.dot(h, win_ref[:, :c_rope], preferred_element_type=F32)
    rope = jnp.dot(h, wrope_ref[...], preferred_element_type=F32)

    zu = jax.nn.gelu(proj[:, :GMLP_WIDTH])
    zv = jax.nn.gelu(proj[:, GMLP_WIDTH:2 * GMLP_WIDTH])
    vn = (zv * _rms_scale(zv, GMLP_WIDTH) * gv_ref[...]).astype(BF16)
    row = lax.broadcasted_iota(jnp.int32, (GMLP_BLOCK, GMLP_BLOCK), 0)
    col = lax.broadcasted_iota(jnp.int32, (GMLP_BLOCK, GMLP_BLOCK), 1)
    tri = col <= row
    for hh in range(GMLP_HEADS):
        w = jnp.where(tri, ws_ref[hh], 0.0).astype(BF16)
        cs = slice(hh * LANES, (hh + 1) * LANES)
        for r in range(tm // GMLP_BLOCK):
            rs = slice(r * GMLP_BLOCK, (r + 1) * GMLP_BLOCK)
            sv = jnp.dot(w, vn[rs, cs], preferred_element_type=F32) + bsb_ref[:, cs]
            oa_ref[rs, cs] = (zu[rs, cs] * sv).astype(BF16)

    c0 = 2 * GMLP_WIDTH
    cq = proj[:, c0:c0 + MLA_Q_RANK]
    ckv = proj[:, c0 + MLA_Q_RANK:c0 + MLA_Q_RANK + MLA_KV_RANK]
    kr = rope[:, :LANES]
    krr = rope[:, LANES:]
    cqn = (cq * _rms_scale(cq, MLA_Q_RANK) * gq_ref[...]).astype(BF16)
    qall = jnp.dot(cqn, wuq_ref[...], preferred_element_type=F32)
    ckvn = (ckv * _rms_scale(ckv, MLA_KV_RANK) * gkv_ref[...]).astype(BF16)
    kvall = jnp.dot(ckvn, wukv_ref[...], preferred_element_type=F32)

    cos_f = cos_t_ref[...] * cos_o_ref[...] - sin_t_ref[...] * sin_o_ref[...]
    sin_f = sins_t_ref[...] * cos_o_ref[...] + cos_t_ref[...] * sins_o_ref[...]
    lane = lax.broadcasted_iota(jnp.int32, (tm, LANES), 1)
    lo = lane < MLA_ROPE
    k_rot = kr * gkr_ref[...] * cos_f + krr * gkrr_ref[...] * sin_f
    ssq_kr = jnp.sum(jnp.where(lo, kr * kr, 0.0), axis=-1, keepdims=True)
    q_scale = MLA_QK ** -0.5
    nh = MLA_HEADS * MLA_NOPE
    for g in range(MLA_HEADS // 2):
        qr = qall[:, nh + g * LANES:nh + (g + 1) * LANES]
        qrr = qall[:, nh + 512 + g * LANES:nh + 512 + (g + 1) * LANES]
        q_rot = qr * gqr_ref[...] * cos_f + qrr * gqrr_ref[...] * sin_f
        qr2 = qr * qr
        ssq_half = (jnp.sum(jnp.where(lo, qr2, 0.0), axis=-1, keepdims=True),
                    jnp.sum(jnp.where(lo, 0.0, qr2), axis=-1, keepdims=True))
        for p in range(2):
            hh = 2 * g + p
            hs = slice(hh * LANES, (hh + 1) * LANES)
            qn = qall[:, hs]
            rinv = lax.rsqrt((jnp.sum(qn * qn, axis=-1, keepdims=True) + ssq_half[p]) * (1.0 / MLA_QK)
                             + RMS_EPS) * q_scale
            mine = lo if p == 0 else jnp.logical_not(lo)
            q_ref[hh, :, :LANES] = (qn * gqn_ref[...] * rinv).astype(BF16)
            q_ref[hh, :, LANES:] = (jnp.where(mine, q_rot, 0.0) * rinv).astype(BF16)
            kn = kvall[:, hs]
            rinvk = lax.rsqrt((jnp.sum(kn * kn, axis=-1, keepdims=True) + ssq_kr) * (1.0 / MLA_QK) + RMS_EPS)
            k_ref[hh, :, :LANES] = (kn * gkn_ref[...] * rinvk).astype(BF16)
            k_ref[hh, :, LANES:] = (k_rot * rinvk).astype(BF16)
            v_ref[hh] = kvall[:, nh + hh * LANES:nh + (hh + 1) * LANES].astype(BF16)


def _attn_kernel(q_ref, k_ref, v_ref, u_ref, pv_ref, o_ref, ub_ref, vt_ref):
    tq = q_ref.shape[1]
    qi = pl.program_id(1)
    ub_ref[...] = u_ref[...].astype(BF16)
    vt_ref[...] = pv_ref[...].T.astype(BF16)
    qs = [q_ref[j] for j in range(ATTN_HEADS)]

    def tile(kt, carry, masked):
        start = pl.multiple_of(kt * TK_ATTN, TK_ATTN)
        if masked:
            q_chunk = (qi * tq + lax.broadcasted_iota(jnp.int32, (tq, 1), 0)) // CHUNK
            k_chunk = (start + lax.broadcasted_iota(jnp.int32, (1, TK_ATTN), 1)) // CHUNK
            allowed = k_chunk <= q_chunk
        out = []
        for j in range(ATTN_HEADS):
            m, l, acc = carry[j]
            kk = k_ref[j, pl.ds(start, TK_ATTN), :]
            vv = v_ref[j, pl.ds(start, TK_ATTN), :]
            s = lax.dot_general(qs[j], kk, (((1,), (1,)), ((), ())), preferred_element_type=F32)
            if masked:
                s = jnp.where(allowed, s, -jnp.inf)
            m_new = jnp.maximum(m, jnp.max(s, axis=-1, keepdims=True))
            alpha = jnp.exp(m - m_new)
            p = jnp.exp(s - m_new)
            l = alpha * l + jnp.sum(p, axis=-1, keepdims=True)
            acc = alpha * acc + jnp.dot(p.astype(BF16), vv, preferred_element_type=F32)
            out.append((m_new, l, acc))
        return tuple(out)

    init = tuple((jnp.full((tq, 1), -jnp.inf, F32), jnp.zeros((tq, 1), F32), jnp.zeros((tq, LANES), F32))
                 for _ in range(ATTN_HEADS))
    n_full = (qi * tq) // TK_ATTN
    carry = lax.fori_loop(0, n_full, lambda kt, c: tile(kt, c, False), init)
    carry = tile(n_full, carry, True)
    for j in range(ATTN_HEADS):
        _, l, acc = carry[j]
        o_ref[:, j * LANES:(j + 1) * LANES] = (acc / l).astype(BF16)


def _topk_ranks(s):
    n = s.shape[0]
    idx = lax.broadcasted_iota(jnp.int32, s.shape, 0)
    rank = jnp.full(s.shape, float(PEER_TOPK), F32)
    vals = []
    for kk in range(PEER_TOPK):
        m = jnp.max(s, axis=0, keepdims=True)
        first = jnp.min(jnp.where(s == m, idx, n), axis=0, keepdims=True)
        sel = idx == first
        rank = jnp.where(sel, float(kk), rank)
        s = jnp.where(sel, -jnp.inf, s)
        vals.append(m)
    return jnp.concatenate(vals, axis=0), rank


def _topk_fast(chains):
    work = []
    for scores, rank_out in chains:
        rank_out[...] = jnp.full(scores.shape, float(PEER_TOPK), F32)
        work.append(scores)
    vals = [[] for _ in chains]
    for kk in range(PEER_TOPK):
        for ci, (_, rank_out) in enumerate(chains):
            m = jnp.max(work[ci], axis=0, keepdims=True)
            hit = work[ci] == m
            pltpu.store(rank_out, jnp.full(work[ci].shape, float(kk), F32), mask=hit)
            work[ci] = jnp.where(hit, -jnp.inf, work[ci])
            vals[ci].append(m)
    taken = [jnp.sum(jnp.where(w == -jnp.inf, 1.0, 0.0), axis=0, keepdims=True) for w in work]
    return [jnp.concatenate(v, axis=0) for v in vals], taken


def _pair_counts(v1, v2):
    t = v1.shape[1]
    row8 = lax.broadcasted_iota(jnp.int32, (8, t), 0)
    cands = [v1[0:1] + v2[0:8], v1[0:1] + v2[8:16]]
    for a in range(1, 8):
        cands.append(jnp.where(row8 < PEER_TOPK // (a + 1), v1[a:a + 1] + v2[0:8], -jnp.inf))
    cands.append(v1[8:16] + v2[0:1])
    work = list(cands)
    cum = jnp.zeros((1, t), F32)
    tau = jnp.zeros((1, t), F32)
    above = jnp.zeros((1, t), F32)
    for _ in range(PEER_TOPK):
        m = jnp.max(functools.reduce(jnp.maximum, work), axis=0, keepdims=True)
        hit = [c == m for c in work]
        cnt = jnp.sum(functools.reduce(jnp.add, [jnp.where(hh, 1.0, 0.0) for hh in hit]), axis=0, keepdims=True)
        active = cum < PEER_TOPK
        tau = jnp.where(active, m, tau)
        above = jnp.where(active, cum, above)
        cum = cum + cnt
        work = [jnp.where(hh, -jnp.inf, c) for hh, c in zip(hit, work)]
    need = PEER_TOPK - above
    rows = [[cands[0], cands[1]]] + [[cands[a + 1]] for a in range(1, 8)]
    n_rows = []
    for a in range(PEER_TOPK):
        cs = rows[a] if a < 8 else [cands[9][a - 8:a - 7]]
        gt = functools.reduce(jnp.add, [jnp.sum(jnp.where(c > tau, 1.0, 0.0), axis=0, keepdims=True) for c in cs])
        eq = functools.reduce(jnp.add, [jnp.sum(jnp.where(c == tau, 1.0, 0.0), axis=0, keepdims=True) for c in cs])
        take = jnp.minimum(eq, jnp.maximum(need, 0.0))
        need = need - eq
        n_rows.append(gt + take)
    return jnp.concatenate(n_rows, axis=0)


def _mid_kernel(x_ref, oa_ref, ob_ref, woa_ref, wob_ref, gffn_ref, wqt_ref, sk_ref,
                x1_ref, xnt_ref, n1_ref, a_ref, r2_ref, b_ref, s_ref, vals_ref, rank_ref):
    tm = x_ref.shape[0]
    x1 = (x_ref[...] + jnp.dot(oa_ref[...], woa_ref[...], preferred_element_type=F32)
          + jnp.dot(ob_ref[...], wob_ref[...], preferred_element_type=F32))
    x1_ref[...] = x1
    xn = x1 * _rms_scale(x1, D_MODEL) * gffn_ref[...]
    xnt = xn.T.astype(BF16)
    xnt_ref[...] = xnt
    qt = jnp.dot(wqt_ref[...], xnt, preferred_element_type=F32).astype(BF16)

    n_hp = 2 * PEER_HEADS
    excess = jnp.zeros((1, LANES), F32)
    for hh in range(PEER_HEADS):
        chains, where = [], []
        for hp in (2 * hh, 2 * hh + 1):
            sp = jnp.dot(sk_ref[hp], qt[hp * LANES:(hp + 1) * LANES, :], preferred_element_type=F32)
            s_ref[hp] = sp
            for st in range(tm // LANES):
                cs = slice(st * LANES, (st + 1) * LANES)
                chains.append((sp[:, cs], rank_ref.at[hp, :, cs]))
                where.append((hp, cs))
        vps, takens = _topk_fast(chains)
        for (hp, cs), vp, taken in zip(where, vps, takens):
            vals_ref[hp, :, cs] = vp
            excess = jnp.maximum(excess, taken - PEER_TOPK)

    @pl.when(jnp.max(excess) > 0.0)
    def _():
        for hp in range(n_hp):
            vp, rp = _topk_ranks(s_ref[hp])
            vals_ref[hp] = vp
            rank_ref[hp] = rp

    rowk = lax.broadcasted_iota(jnp.int32, (PEER_TOPK, tm), 0).astype(F32)
    for hh in range(PEER_HEADS):
        v1, v2 = vals_ref[2 * hh], vals_ref[2 * hh + 1]
        n = _pair_counts(v1, v2)
        e1 = jnp.exp(v1 - v1[0:1])
        e2 = jnp.exp(v2 - v2[0:1])
        paired = jnp.zeros((PEER_TOPK, tm), F32)
        for a in range(PEER_TOPK):
            paired = paired + jnp.where(rowk < n[a:a + 1], e1[a:a + 1], 0.0)
        z = jnp.sum(paired * e2, axis=0, keepdims=True)
        rank1 = rank_ref[2 * hh]
        n1 = jnp.zeros((PEER_N_KEYS, tm), F32)
        for a in range(PEER_TOPK):
            n1 = jnp.where(rank1 == float(a), n[a:a + 1], n1)
        n1_ref[hh] = n1
        a_ref[hh] = jnp.exp(s_ref[2 * hh] - v1[0:1]) / z
        r2_ref[hh] = rank_ref[2 * hh + 1].astype(BF16)
        b_ref[hh] = jnp.exp(s_ref[2 * hh + 1] - v2[0:1]).astype(BF16)


def _peer_kernel(n1_ref, a_ref, r2_ref, b_ref, xnt_ref, u_ref, vt_ref, x1_ref, y_ref, acc_ref, g_ref):
    e = pl.program_id(1)
    tt = xnt_ref.shape[1]
    keys = PEER_SUB // PEER_N_KEYS
    pack = 16
    reps = (PEER_N_KEYS // pack, 1)

    @pl.when(e == 0)
    def _():
        acc_ref[...] = jnp.zeros_like(acc_ref)

    def weights(k):
        w = None
        for hh in range(PEER_HEADS):
            n16 = jnp.broadcast_to(n1_ref[hh, k:k + 1, :], (pack, tt)).astype(BF16)
            a16 = jnp.broadcast_to(a_ref[hh, k:k + 1, :], (pack, tt)).astype(BF16)
            term = jnp.where(r2_ref[hh] < jnp.tile(n16, reps), b_ref[hh], jnp.zeros((), BF16)) * jnp.tile(a16, reps)
            w = term if w is None else w + term
        return w

    def activations(c):
        rows = slice(c * PEER_SUB, (c + 1) * PEER_SUB)
        at = jnp.dot(u_ref[rows, :], xnt_ref[...], preferred_element_type=F32)
        g_ref[c % 2] = jax.nn.gelu(at).astype(BF16)

    def accumulate(c):
        rows = slice(c * PEER_SUB, (c + 1) * PEER_SUB)
        wa = jnp.concatenate([g_ref[c % 2, k * PEER_N_KEYS:(k + 1) * PEER_N_KEYS, :] * weights(c * keys + k)
                              for k in range(keys)], axis=0)
        acc_ref[...] += jnp.dot(vt_ref[:, rows], wa, preferred_element_type=F32)

    n_sub = EC_PEER // PEER_SUB
    activations(0)
    for c in range(n_sub):
        if c + 1 < n_sub:
            activations(c + 1)
        accumulate(c)

    @pl.when(e == pl.num_programs(1) - 1)
    def _():
        y_ref[...] = x1_ref[...] + acc_ref[...].T


def _const_spec(shape):
    zeros = (0,) * len(shape)
    return pl.BlockSpec(shape, lambda *_: zeros)


def _params(semantics):
    return pltpu.CompilerParams(dimension_semantics=semantics, vmem_limit_bytes=VMEM_LIMIT)


def kernel(x, norm_mix_g, w_in, gmlp_vnorm_g, gmlp_ws, gmlp_bs, mla_qnorm_g, mla_w_uq, mla_kvnorm_g, mla_w_ukv,
           mla_qk_q_g, mla_qk_k_g, w_out, norm_ffn_g, peer_w_q, peer_subkeys, peer_u, peer_v):
    bsz, seq, d = x.shape
    assert bsz == 1 and d == D_MODEL and seq % TK_ATTN == 0 and TK_ATTN % TQ_ATTN == 0 and seq % TT_PEER == 0
    assert peer_u.shape[1] % ((MLA_HEADS // ATTN_HEADS) * (seq // TQ_ATTN) * LANES) == 0
    x2 = x[0]
    n_exp = peer_u.shape[1]

    rot = (np.arange(MLA_ROPE) + MLA_ROPE // 2) % MLA_ROPE
    c_rope = 2 * GMLP_WIDTH + MLA_Q_RANK + MLA_KV_RANK
    w_in_bf = w_in[0].astype(BF16)
    w_rope = w_in_bf[:, c_rope:]
    w_rope_ext = jnp.concatenate([w_rope, w_rope, w_rope[:, rot], w_rope[:, rot]], axis=1)
    wq3 = mla_w_uq[0].reshape(MLA_Q_RANK, MLA_HEADS, MLA_QK)
    wq_rope = wq3[:, :, MLA_NOPE:]
    w_uq_ext = jnp.concatenate([wq3[:, :, :MLA_NOPE].reshape(MLA_Q_RANK, -1),
                                wq_rope.reshape(MLA_Q_RANK, -1),
                                wq_rope[:, :, rot].reshape(MLA_Q_RANK, -1)], axis=1).astype(BF16)
    wkv3 = mla_w_ukv[0].reshape(MLA_KV_RANK, MLA_HEADS, MLA_NOPE + LANES)
    w_ukv_ext = jnp.concatenate([wkv3[:, :, :MLA_NOPE].reshape(MLA_KV_RANK, -1),
                                 wkv3[:, :, MLA_NOPE:].reshape(MLA_KV_RANK, -1)], axis=1).astype(BF16)
    row = lambda v: v.reshape(1, -1)
    gq, gk = mla_qk_q_g[0], mla_qk_k_g[0]
    dup = lambda v: row(jnp.concatenate([v, v]))
    bsb = jnp.repeat(gmlp_bs[0].T, LANES, axis=1)
    inv_freq = ROPE_THETA ** (-jnp.arange(0, MLA_ROPE, 2, dtype=F32) / MLA_ROPE)
    n_front = seq // TM_FRONT
    lanes4 = lambda t: jnp.concatenate([t, t, t, t], axis=1)
    signed = lambda t: jnp.concatenate([-t, t, -t, t], axis=1)
    ang_t = (jnp.arange(n_front, dtype=F32) * TM_FRONT)[:, None] * inv_freq[None, :]
    ang_o = jnp.arange(TM_FRONT, dtype=F32)[:, None] * inv_freq[None, :]
    tile3 = lambda t: t.reshape(n_front, 1, LANES)
    cos_t, sin_t, sins_t = tile3(lanes4(jnp.cos(ang_t))), tile3(lanes4(jnp.sin(ang_t))), tile3(signed(jnp.sin(ang_t)))
    cos_o, sin_o, sins_o = lanes4(jnp.cos(ang_o)), lanes4(jnp.sin(ang_o)), signed(jnp.sin(ang_o))

    rows = lambda w: pl.BlockSpec((TM_FRONT, w), lambda i: (i, 0))
    per_tile = pl.BlockSpec((None, 1, LANES), lambda i: (i, 0, 0))
    heads = lambda w: pl.BlockSpec((MLA_HEADS, TM_FRONT, w), lambda i: (0, i, 0))
    oa, q, k, v = pl.pallas_call(
        _front_kernel,
        grid=(n_front,),
        in_specs=[rows(D_MODEL), _const_spec((1, D_MODEL)), _const_spec(w_in_bf.shape), _const_spec(w_rope_ext.shape),
                  _const_spec((1, GMLP_WIDTH)), _const_spec(gmlp_ws[0].shape), _const_spec(bsb.shape),
                  _const_spec((1, MLA_Q_RANK)), _const_spec(w_uq_ext.shape),
                  _const_spec((1, MLA_KV_RANK)), _const_spec(w_ukv_ext.shape),
                  _const_spec((1, LANES)), _const_spec((1, LANES)), _const_spec((1, LANES)),
                  _const_spec((1, LANES)), _const_spec((1, LANES)), _const_spec((1, LANES)),
                  per_tile, per_tile, per_tile,
                  _const_spec((TM_FRONT, LANES)), _const_spec((TM_FRONT, LANES)), _const_spec((TM_FRONT, LANES))],
        out_specs=[rows(GMLP_WIDTH), heads(QK_PAD), heads(QK_PAD), heads(LANES)],
        out_shape=[jax.ShapeDtypeStruct((seq, GMLP_WIDTH), BF16),
                   jax.ShapeDtypeStruct((MLA_HEADS, seq, QK_PAD), BF16),
                   jax.ShapeDtypeStruct((MLA_HEADS, seq, QK_PAD), BF16),
                   jax.ShapeDtypeStruct((MLA_HEADS, seq, LANES), BF16)],
        compiler_params=_params(("parallel",)),
        name="front",
    )(x2, row(norm_mix_g[0]), w_in_bf, w_rope_ext, row(gmlp_vnorm_g[0]), gmlp_ws[0], bsb,
      row(mla_qnorm_g[0]), w_uq_ext, row(mla_kvnorm_g[0]), w_ukv_ext,
      row(gq[:MLA_NOPE]), dup(gq[MLA_NOPE:]), dup(gq[MLA_NOPE:][rot]),
      row(gk[:MLA_NOPE]), dup(gk[MLA_NOPE:]), dup(gk[MLA_NOPE:][rot]),
      cos_t, sin_t, sins_t, cos_o, sin_o, sins_o)

    n_q = seq // TQ_ATTN
    n_attn_steps = (MLA_HEADS // ATTN_HEADS) * n_q
    slab = n_exp // n_attn_steps
    ob, u_bf, vt_bf = pl.pallas_call(
        _attn_kernel,
        grid=(MLA_HEADS // ATTN_HEADS, n_q),
        in_specs=[pl.BlockSpec((ATTN_HEADS, TQ_ATTN, QK_PAD), lambda h, i: (h, i, 0)),
                  pl.BlockSpec((ATTN_HEADS, seq, QK_PAD), lambda h, i: (h, 0, 0)),
                  pl.BlockSpec((ATTN_HEADS, seq, LANES), lambda h, i: (h, 0, 0)),
                  pl.BlockSpec((slab, D_MODEL), lambda h, i: (h * n_q + i, 0)),
                  pl.BlockSpec((slab, D_MODEL), lambda h, i: (h * n_q + i, 0))],
        out_specs=[pl.BlockSpec((TQ_ATTN, ATTN_HEADS * LANES), lambda h, i: (i, h)),
                   pl.BlockSpec((slab, D_MODEL), lambda h, i: (h * n_q + i, 0)),
                   pl.BlockSpec((D_MODEL, slab), lambda h, i: (0, h * n_q + i))],
        out_shape=[jax.ShapeDtypeStruct((seq, MLA_HEADS * LANES), BF16),
                   jax.ShapeDtypeStruct((n_exp, D_MODEL), BF16),
                   jax.ShapeDtypeStruct((D_MODEL, n_exp), BF16)],
        compiler_params=_params(("parallel", "parallel")),
        name="attn",
    )(q, k, v, peer_u[0], peer_v[0])

    wo = w_out[0].astype(BF16)
    wqt = peer_w_q[0].T.astype(BF16)
    sk = peer_subkeys[0].reshape(PEER_HEADS * 2, PEER_N_KEYS, -1).astype(BF16)
    n_mid = seq // TM_MID
    mrows = lambda w: pl.BlockSpec((TM_MID, w), lambda i: (i, 0))
    tok = pl.BlockSpec((PEER_HEADS, PEER_N_KEYS, TM_MID), lambda i: (0, 0, i))
    tok_shape = lambda dt: jax.ShapeDtypeStruct((PEER_HEADS, PEER_N_KEYS, seq), dt)
    x1, xnt, n1, a1, r2, b2 = pl.pallas_call(
        _mid_kernel,
        grid=(n_mid,),
        in_specs=[mrows(D_MODEL), mrows(GMLP_WIDTH), mrows(MLA_HEADS * LANES),
                  _const_spec((GMLP_WIDTH, D_MODEL)), _const_spec((MLA_HEADS * LANES, D_MODEL)),
                  _const_spec((1, D_MODEL)), _const_spec(wqt.shape), _const_spec(sk.shape)],
        out_specs=[mrows(D_MODEL), pl.BlockSpec((D_MODEL, TM_MID), lambda i: (0, i)), tok, tok, tok, tok],
        out_shape=[jax.ShapeDtypeStruct((seq, D_MODEL), F32), jax.ShapeDtypeStruct((D_MODEL, seq), BF16),
                   tok_shape(F32), tok_shape(F32), tok_shape(BF16), tok_shape(BF16)],
        scratch_shapes=[pltpu.VMEM((2 * PEER_HEADS, PEER_N_KEYS, TM_MID), F32),
                        pltpu.VMEM((2 * PEER_HEADS, PEER_TOPK, TM_MID), F32),
                        pltpu.VMEM((2 * PEER_HEADS, PEER_N_KEYS, TM_MID), F32)],
        compiler_params=_params(("parallel",)),
        name="mid",
    )(x2, oa, ob, wo[:GMLP_WIDTH], wo[GMLP_WIDTH:], row(norm_ffn_g[0]), wqt, sk)

    keys_per_step = EC_PEER // PEER_N_KEYS
    first_half = pl.BlockSpec((PEER_HEADS, keys_per_step, TT_PEER), lambda t, e: (0, e, t))
    second_half = pl.BlockSpec((PEER_HEADS, PEER_N_KEYS, TT_PEER), lambda t, e: (0, 0, t))
    y = pl.pallas_call(
        _peer_kernel,
        grid=(seq // TT_PEER, n_exp // EC_PEER),
        in_specs=[first_half, first_half, second_half, second_half,
                  pl.BlockSpec((D_MODEL, TT_PEER), lambda t, e: (0, t)),
                  pl.BlockSpec((EC_PEER, D_MODEL), lambda t, e: (e, 0)),
                  pl.BlockSpec((D_MODEL, EC_PEER), lambda t, e: (0, e)),
                  pl.BlockSpec((TT_PEER, D_MODEL), lambda t, e: (t, 0))],
        out_specs=pl.BlockSpec((TT_PEER, D_MODEL), lambda t, e: (t, 0)),
        out_shape=jax.ShapeDtypeStruct((seq, D_MODEL), F32),
        scratch_shapes=[pltpu.VMEM((D_MODEL, TT_PEER), F32), pltpu.VMEM((2, PEER_SUB, TT_PEER), BF16)],
        compiler_params=_params(("parallel", "arbitrary")),
        name="peer",
    )(n1, a1, r2, b2, xnt, u_bf, vt_bf, x1)
    return y[None]
```

```python
import functools

import jax
import jax.numpy as jnp
import numpy as np
from jax import lax
from jax.experimental import pallas as pl
from jax.experimental.pallas import tpu as pltpu

F32 = jnp.float32
BF16 = jnp.bfloat16

D_MODEL = 2048
CHUNK = 64
GMLP_HEADS = 8
GMLP_WIDTH = 1024
GMLP_BLOCK = 128
MLA_HEADS = 8
MLA_NOPE = 128
MLA_ROPE = 64
MLA_QK = MLA_NOPE + MLA_ROPE
MLA_Q_RANK = 512
MLA_KV_RANK = 256
ROPE_THETA = 10000.0
PEER_HEADS = 8
PEER_N_KEYS = 128
PEER_TOPK = 16
RMS_EPS = 1e-6
LANES = 128
QK_PAD = 2 * LANES

TM_FRONT = 256
TQ_ATTN = 512
TK_ATTN = 1024
ATTN_HEADS = 2
TM_MID = 256
TT_PEER = 512
EC_PEER = 1024
PEER_SUB = 512
VMEM_LIMIT = 56 * 1024 * 1024


def _rms_scale(v, width):
    return lax.rsqrt(jnp.sum(v * v, axis=-1, keepdims=True) * (1.0 / width) + RMS_EPS)


def _front_kernel(x_ref, gmix_ref, win_ref, wrope_ref, gv_ref, ws_ref, bsb_ref, gq_ref, wuq_ref, gkv_ref, wukv_ref,
                  gqn_ref, gqr_ref, gqrr_ref, gkn_ref, gkr_ref, gkrr_ref,
                  cos_t_ref, sin_t_ref, sins_t_ref, cos_o_ref, sin_o_ref, sins_o_ref,
                  oa_ref, q_ref, k_ref, v_ref):
    tm = x_ref.shape[0]
    x = x_ref[...]
    h = (x * _rms_scale(x, D_MODEL) * gmix_ref[...]).astype(BF16)
    c_rope = 2 * GMLP_WIDTH + MLA_Q_RANK + MLA_KV_RANK
    proj = jnp.dot(h, win_ref[:, :c_rope], preferred_element_type=F32)
    rope = jnp.dot(h, wrope_ref[...], preferred_element_type=F32)

    zu = jax.nn.gelu(proj[:, :GMLP_WIDTH])
    zv = jax.nn.gelu(proj[:, GMLP_WIDTH:2 * GMLP_WIDTH])
    vn = (zv * _rms_scale(zv, GMLP_WIDTH) * gv_ref[...]).astype(BF16)
    row = lax.broadcasted_iota(jnp.int32, (GMLP_BLOCK, GMLP_BLOCK), 0)
    col = lax.broadcasted_iota(jnp.int32, (GMLP_BLOCK, GMLP_BLOCK), 1)
    tri = col <= row
    for hh in range(GMLP_HEADS):
        w = jnp.where(tri, ws_ref[hh], 0.0).astype(BF16)
        cs = slice(hh * LANES, (hh + 1) * LANES)
        for r in range(tm // GMLP_BLOCK):
            rs = slice(r * GMLP_BLOCK, (r + 1) * GMLP_BLOCK)
            sv = jnp.dot(w, vn[rs, cs], preferred_element_type=F32) + bsb_ref[:, cs]
            oa_ref[rs, cs] = (zu[rs, cs] * sv).astype(BF16)

    c0 = 2 * GMLP_WIDTH
    cq = proj[:, c0:c0 + MLA_Q_RANK]
    ckv = proj[:, c0 + MLA_Q_RANK:c0 + MLA_Q_RANK + MLA_KV_RANK]
    kr = rope[:, :LANES]
    krr = rope[:, LANES:]
    cqn = (cq * _rms_scale(cq, MLA_Q_RANK) * gq_ref[...]).astype(BF16)
    qall = jnp.dot(cqn, wuq_ref[...], preferred_element_type=F32)
    ckvn = (ckv * _rms_scale(ckv, MLA_KV_RANK) * gkv_ref[...]).astype(BF16)
    kvall = jnp.dot(ckvn, wukv_ref[...], preferred_element_type=F32)

    cos_f = cos_t_ref[...] * cos_o_ref[...] - sin_t_ref[...] * sin_o_ref[...]
    sin_f = sins_t_ref[...] * cos_o_ref[...] + cos_t_ref[...] * sins_o_ref[...]
    lane = lax.broadcasted_iota(jnp.int32, (tm, LANES), 1)
    lo = lane < MLA_ROPE
    k_rot = kr * gkr_ref[...] * cos_f + krr * gkrr_ref[...] * sin_f
    ssq_kr = jnp.sum(jnp.where(lo, kr * kr, 0.0), axis=-1, keepdims=True)
    q_scale = MLA_QK ** -0.5
    nh = MLA_HEADS * MLA_NOPE
    for g in range(MLA_HEADS // 2):
        qr = qall[:, nh + g * LANES:nh + (g + 1) * LANES]
        qrr = qall[:, nh + 512 + g * LANES:nh + 512 + (g + 1) * LANES]
        q_rot = qr * gqr_ref[...] * cos_f + qrr * gqrr_ref[...] * sin_f
        qr2 = qr * qr
        ssq_half = (jnp.sum(jnp.where(lo, qr2, 0.0), axis=-1, keepdims=True),
                    jnp.sum(jnp.where(lo, 0.0, qr2), axis=-1, keepdims=True))
        for p in range(2):
            hh = 2 * g + p
            hs = slice(hh * LANES, (hh + 1) * LANES)
            qn = qall[:, hs]
            rinv = lax.rsqrt((jnp.sum(qn * qn, axis=-1, keepdims=True) + ssq_half[p]) * (1.0 / MLA_QK)
                             + RMS_EPS) * q_scale
            mine = lo if p == 0 else jnp.logical_not(lo)
            q_ref[hh, :, :LANES] = (qn * gqn_ref[...] * rinv).astype(BF16)
            q_ref[hh, :, LANES:] = (jnp.where(mine, q_rot, 0.0) * rinv).astype(BF16)
            kn = kvall[:, hs]
            rinvk = lax.rsqrt((jnp.sum(kn * kn, axis=-1, keepdims=True) + ssq_kr) * (1.0 / MLA_QK) + RMS_EPS)
            k_ref[hh, :, :LANES] = (kn * gkn_ref[...] * rinvk).astype(BF16)
            k_ref[hh, :, LANES:] = (k_rot * rinvk).astype(BF16)
            v_ref[hh] = kvall[:, nh + hh * LANES:nh + (hh + 1) * LANES].astype(BF16)


def _attn_kernel(q_ref, k_ref, v_ref, u_ref, pv_ref, wq_ref, wo_ref, o_ref, ub_ref, vt_ref, wqt_ref, wob_ref, *,
                 share):
    tq = q_ref.shape[1]
    qi = pl.program_id(1)
    ub_ref[...] = u_ref[...].astype(BF16)
    vt_ref[...] = pv_ref[...].T.astype(BF16)

    @pl.when((pl.program_id(0) * pl.num_programs(1) + qi) % share == 0)
    def _():
        wqt_ref[...] = wq_ref[...].T.astype(BF16)
        wob_ref[...] = wo_ref[...].astype(BF16)
    qs = [q_ref[j] for j in range(ATTN_HEADS)]

    def tile(kt, carry, masked):
        start = pl.multiple_of(kt * TK_ATTN, TK_ATTN)
        if masked:
            q_chunk = (qi * tq + lax.broadcasted_iota(jnp.int32, (tq, 1), 0)) // CHUNK
            k_chunk = (start + lax.broadcasted_iota(jnp.int32, (1, TK_ATTN), 1)) // CHUNK
            allowed = k_chunk <= q_chunk
        out = []
        for j in range(ATTN_HEADS):
            m, l, acc = carry[j]
            kk = k_ref[j, pl.ds(start, TK_ATTN), :]
            vv = v_ref[j, pl.ds(start, TK_ATTN), :]
            s = lax.dot_general(qs[j], kk, (((1,), (1,)), ((), ())), preferred_element_type=F32)
            if masked:
                s = jnp.where(allowed, s, -jnp.inf)
            m_new = jnp.maximum(m, jnp.max(s, axis=-1, keepdims=True))
            alpha = jnp.exp(m - m_new)
            p = jnp.exp(s - m_new)
            l = alpha * l + jnp.sum(p, axis=-1, keepdims=True)
            acc = alpha * acc + jnp.dot(p.astype(BF16), vv, preferred_element_type=F32)
            out.append((m_new, l, acc))
        return tuple(out)

    init = tuple((jnp.full((tq, 1), -jnp.inf, F32), jnp.zeros((tq, 1), F32), jnp.zeros((tq, LANES), F32))
                 for _ in range(ATTN_HEADS))
    n_full = (qi * tq) // TK_ATTN
    carry = lax.fori_loop(0, n_full, lambda kt, c: tile(kt, c, False), init)
    carry = tile(n_full, carry, True)
    for j in range(ATTN_HEADS):
        _, l, acc = carry[j]
        o_ref[:, j * LANES:(j + 1) * LANES] = (acc / l).astype(BF16)


def _topk_ranks(s):
    n = s.shape[0]
    idx = lax.broadcasted_iota(jnp.int32, s.shape, 0)
    rank = jnp.full(s.shape, float(PEER_TOPK), F32)
    vals = []
    for kk in range(PEER_TOPK):
        m = jnp.max(s, axis=0, keepdims=True)
        first = jnp.min(jnp.where(s == m, idx, n), axis=0, keepdims=True)
        sel = idx == first
        rank = jnp.where(sel, float(kk), rank)
        s = jnp.where(sel, -jnp.inf, s)
        vals.append(m)
    return jnp.concatenate(vals, axis=0), rank


def _topk_fast(s):
    rank = jnp.full(s.shape, float(PEER_TOPK), F32)
    vals = []
    for kk in range(PEER_TOPK):
        m = jnp.max(s, axis=0, keepdims=True)
        hit = s == m
        rank = jnp.where(hit, float(kk), rank)
        s = jnp.where(hit, -jnp.inf, s)
        vals.append(m)
    taken = jnp.sum(jnp.where(rank < PEER_TOPK, 1.0, 0.0), axis=0, keepdims=True)
    return jnp.concatenate(vals, axis=0), rank, taken


def _pair_counts(v1, v2):
    t = v1.shape[1]
    row8 = lax.broadcasted_iota(jnp.int32, (8, t), 0)
    cands = [v1[0:1] + v2[0:8], v1[0:1] + v2[8:16]]
    for a in range(1, 8):
        cands.append(jnp.where(row8 < PEER_TOPK // (a + 1), v1[a:a + 1] + v2[0:8], -jnp.inf))
    cands.append(v1[8:16] + v2[0:1])
    work = list(cands)
    cum = jnp.zeros((1, t), F32)
    tau = jnp.zeros((1, t), F32)
    above = jnp.zeros((1, t), F32)
    for _ in range(PEER_TOPK):
        m = jnp.max(functools.reduce(jnp.maximum, work), axis=0, keepdims=True)
        hit = [c == m for c in work]
        cnt = jnp.sum(functools.reduce(jnp.add, [jnp.where(hh, 1.0, 0.0) for hh in hit]), axis=0, keepdims=True)
        active = cum < PEER_TOPK
        tau = jnp.where(active, m, tau)
        above = jnp.where(active, cum, above)
        cum = cum + cnt
        work = [jnp.where(hh, -jnp.inf, c) for hh, c in zip(hit, work)]
    need = PEER_TOPK - above
    rows = [[cands[0], cands[1]]] + [[cands[a + 1]] for a in range(1, 8)]
    n_rows = []
    for a in range(PEER_TOPK):
        cs = rows[a] if a < 8 else [cands[9][a - 8:a - 7]]
        gt = functools.reduce(jnp.add, [jnp.sum(jnp.where(c > tau, 1.0, 0.0), axis=0, keepdims=True) for c in cs])
        eq = functools.reduce(jnp.add, [jnp.sum(jnp.where(c == tau, 1.0, 0.0), axis=0, keepdims=True) for c in cs])
        take = jnp.minimum(eq, jnp.maximum(need, 0.0))
        need = need - eq
        n_rows.append(gt + take)
    return jnp.concatenate(n_rows, axis=0)


def _mid_kernel(x_ref, oa_ref, ob_ref, woa_ref, wob_ref, gffn_ref, wqt_ref, sk_ref,
                x1_ref, xnt_ref, n1_ref, a_ref, r2_ref, b_ref, s_ref, vals_ref, rank_ref):
    tm = x_ref.shape[0]
    x1 = (x_ref[...] + jnp.dot(oa_ref[...], woa_ref[...], preferred_element_type=F32)
          + jnp.dot(ob_ref[...], wob_ref[...], preferred_element_type=F32))
    x1_ref[...] = x1
    xn = x1 * _rms_scale(x1, D_MODEL) * gffn_ref[...]
    xnt = xn.T.astype(BF16)
    xnt_ref[...] = xnt
    qt = jnp.dot(wqt_ref[...], xnt, preferred_element_type=F32).astype(BF16)

    n_hp = 2 * PEER_HEADS
    excess = jnp.zeros((1, tm), F32)
    for hp in range(n_hp):
        sp = jnp.dot(sk_ref[hp], qt[hp * LANES:(hp + 1) * LANES, :], preferred_element_type=F32)
        s_ref[hp] = sp
        vp, rp, taken = _topk_fast(sp)
        vals_ref[hp] = vp
        rank_ref[hp] = rp
        excess = jnp.maximum(excess, taken - PEER_TOPK)

    @pl.when(jnp.max(excess) > 0.0)
    def _():
        for hp in range(n_hp):
            vp, rp = _topk_ranks(s_ref[hp])
            vals_ref[hp] = vp
            rank_ref[hp] = rp

    rowk = lax.broadcasted_iota(jnp.int32, (PEER_TOPK, tm), 0).astype(F32)
    for hh in range(PEER_HEADS):
        v1, v2 = vals_ref[2 * hh], vals_ref[2 * hh + 1]
        n = _pair_counts(v1, v2)
        e1 = jnp.exp(v1 - v1[0:1])
        e2 = jnp.exp(v2 - v2[0:1])
        paired = jnp.zeros((PEER_TOPK, tm), F32)
        for a in range(PEER_TOPK):
            paired = paired + jnp.where(rowk < n[a:a + 1], e1[a:a + 1], 0.0)
        z = jnp.sum(paired * e2, axis=0, keepdims=True)
        rank1 = rank_ref[2 * hh]
        n1 = jnp.zeros((PEER_N_KEYS, tm), F32)
        for a in range(PEER_TOPK):
            n1 = jnp.where(rank1 == float(a), n[a:a + 1], n1)
        n1_ref[hh] = n1
        a_ref[hh] = jnp.exp(s_ref[2 * hh] - v1[0:1]) / z
        r2_ref[hh] = rank_ref[2 * hh + 1].astype(BF16)
        b_ref[hh] = jnp.exp(s_ref[2 * hh + 1] - v2[0:1]).astype(BF16)


def _peer_kernel(n1_ref, a_ref, r2_ref, b_ref, xnt_ref, u_ref, vt_ref, x1_ref, y_ref, acc_ref, g_ref):
    e = pl.program_id(1)
    tt = xnt_ref.shape[1]
    keys = PEER_SUB // PEER_N_KEYS
    pack = 16
    reps = (PEER_N_KEYS // pack, 1)

    @pl.when(e == 0)
    def _():
        acc_ref[...] = jnp.zeros_like(acc_ref)

    def weights(k):
        w = None
        for hh in range(PEER_HEADS):
            n16 = jnp.broadcast_to(n1_ref[hh, k:k + 1, :], (pack, tt)).astype(BF16)
            a16 = jnp.broadcast_to(a_ref[hh, k:k + 1, :], (pack, tt)).astype(BF16)
            term = jnp.where(r2_ref[hh] < jnp.tile(n16, reps), b_ref[hh], jnp.zeros((), BF16)) * jnp.tile(a16, reps)
            w = term if w is None else w + term
        return w

    def activations(c):
        rows = slice(c * PEER_SUB, (c + 1) * PEER_SUB)
        at = jnp.dot(u_ref[rows, :], xnt_ref[...], preferred_element_type=F32)
        g_ref[c % 2] = jax.nn.gelu(at).astype(BF16)

    def accumulate(c):
        rows = slice(c * PEER_SUB, (c + 1) * PEER_SUB)
        wa = jnp.concatenate([g_ref[c % 2, k * PEER_N_KEYS:(k + 1) * PEER_N_KEYS, :] * weights(c * keys + k)
                              for k in range(keys)], axis=0)
        acc_ref[...] += jnp.dot(vt_ref[:, rows], wa, preferred_element_type=F32)

    n_sub = EC_PEER // PEER_SUB
    activations(0)
    for c in range(n_sub):
        if c + 1 < n_sub:
            activations(c + 1)
        accumulate(c)

    @pl.when(e == pl.num_programs(1) - 1)
    def _():
        y_ref[...] = x1_ref[...] + acc_ref[...].T


def _const_spec(shape):
    zeros = (0,) * len(shape)
    return pl.BlockSpec(shape, lambda *_: zeros)


def _params(semantics):
    return pltpu.CompilerParams(dimension_semantics=semantics, vmem_limit_bytes=VMEM_LIMIT)


def kernel(x, norm_mix_g, w_in, gmlp_vnorm_g, gmlp_ws, gmlp_bs, mla_qnorm_g, mla_w_uq, mla_kvnorm_g, mla_w_ukv,
           mla_qk_q_g, mla_qk_k_g, w_out, norm_ffn_g, peer_w_q, peer_subkeys, peer_u, peer_v):
    bsz, seq, d = x.shape
    assert bsz == 1 and d == D_MODEL and seq % TK_ATTN == 0 and TK_ATTN % TQ_ATTN == 0 and seq % TT_PEER == 0
    assert peer_u.shape[1] % ((MLA_HEADS // ATTN_HEADS) * (seq // TQ_ATTN) * LANES) == 0
    x2 = x[0]
    n_exp = peer_u.shape[1]

    rot = (np.arange(MLA_ROPE) + MLA_ROPE // 2) % MLA_ROPE
    c_rope = 2 * GMLP_WIDTH + MLA_Q_RANK + MLA_KV_RANK
    w_in_bf = w_in[0].astype(BF16)
    w_rope = w_in_bf[:, c_rope:]
    w_rope_ext = jnp.concatenate([w_rope, w_rope, w_rope[:, rot], w_rope[:, rot]], axis=1)
    wq3 = mla_w_uq[0].reshape(MLA_Q_RANK, MLA_HEADS, MLA_QK)
    wq_rope = wq3[:, :, MLA_NOPE:]
    w_uq_ext = jnp.concatenate([wq3[:, :, :MLA_NOPE].reshape(MLA_Q_RANK, -1),
                                wq_rope.reshape(MLA_Q_RANK, -1),
                                wq_rope[:, :, rot].reshape(MLA_Q_RANK, -1)], axis=1).astype(BF16)
    wkv3 = mla_w_ukv[0].reshape(MLA_KV_RANK, MLA_HEADS, MLA_NOPE + LANES)
    w_ukv_ext = jnp.concatenate([wkv3[:, :, :MLA_NOPE].reshape(MLA_KV_RANK, -1),
                                 wkv3[:, :, MLA_NOPE:].reshape(MLA_KV_RANK, -1)], axis=1).astype(BF16)
    row = lambda v: v.reshape(1, -1)
    gq, gk = mla_qk_q_g[0], mla_qk_k_g[0]
    dup = lambda v: row(jnp.concatenate([v, v]))
    bsb = jnp.repeat(gmlp_bs[0].T, LANES, axis=1)
    inv_freq = ROPE_THETA ** (-jnp.arange(0, MLA_ROPE, 2, dtype=F32) / MLA_ROPE)
    n_front = seq // TM_FRONT
    lanes4 = lambda t: jnp.concatenate([t, t, t, t], axis=1)
    signed = lambda t: jnp.concatenate([-t, t, -t, t], axis=1)
    ang_t = (jnp.arange(n_front, dtype=F32) * TM_FRONT)[:, None] * inv_freq[None, :]
    ang_o = jnp.arange(TM_FRONT, dtype=F32)[:, None] * inv_freq[None, :]
    tile3 = lambda t: t.reshape(n_front, 1, LANES)
    cos_t, sin_t, sins_t = tile3(lanes4(jnp.cos(ang_t))), tile3(lanes4(jnp.sin(ang_t))), tile3(signed(jnp.sin(ang_t)))
    cos_o, sin_o, sins_o = lanes4(jnp.cos(ang_o)), lanes4(jnp.sin(ang_o)), signed(jnp.sin(ang_o))

    rows = lambda w: pl.BlockSpec((TM_FRONT, w), lambda i: (i, 0))
    per_tile = pl.BlockSpec((None, 1, LANES), lambda i: (i, 0, 0))
    heads = lambda w: pl.BlockSpec((MLA_HEADS, TM_FRONT, w), lambda i: (0, i, 0))
    oa, q, k, v = pl.pallas_call(
        _front_kernel,
        grid=(n_front,),
        in_specs=[rows(D_MODEL), _const_spec((1, D_MODEL)), _const_spec(w_in_bf.shape), _const_spec(w_rope_ext.shape),
                  _const_spec((1, GMLP_WIDTH)), _const_spec(gmlp_ws[0].shape), _const_spec(bsb.shape),
                  _const_spec((1, MLA_Q_RANK)), _const_spec(w_uq_ext.shape),
                  _const_spec((1, MLA_KV_RANK)), _const_spec(w_ukv_ext.shape),
                  _const_spec((1, LANES)), _const_spec((1, LANES)), _const_spec((1, LANES)),
                  _const_spec((1, LANES)), _const_spec((1, LANES)), _const_spec((1, LANES)),
                  per_tile, per_tile, per_tile,
                  _const_spec((TM_FRONT, LANES)), _const_spec((TM_FRONT, LANES)), _const_spec((TM_FRONT, LANES))],
        out_specs=[rows(GMLP_WIDTH), heads(QK_PAD), heads(QK_PAD), heads(LANES)],
        out_shape=[jax.ShapeDtypeStruct((seq, GMLP_WIDTH), BF16),
                   jax.ShapeDtypeStruct((MLA_HEADS, seq, QK_PAD), BF16),
                   jax.ShapeDtypeStruct((MLA_HEADS, seq, QK_PAD), BF16),
                   jax.ShapeDtypeStruct((MLA_HEADS, seq, LANES), BF16)],
        compiler_params=_params(("parallel",)),
        name="front",
    )(x2, row(norm_mix_g[0]), w_in_bf, w_rope_ext, row(gmlp_vnorm_g[0]), gmlp_ws[0], bsb,
      row(mla_qnorm_g[0]), w_uq_ext, row(mla_kvnorm_g[0]), w_ukv_ext,
      row(gq[:MLA_NOPE]), dup(gq[MLA_NOPE:]), dup(gq[MLA_NOPE:][rot]),
      row(gk[:MLA_NOPE]), dup(gk[MLA_NOPE:]), dup(gk[MLA_NOPE:][rot]),
      cos_t, sin_t, sins_t, cos_o, sin_o, sins_o)

    n_q = seq // TQ_ATTN
    n_attn_steps = (MLA_HEADS // ATTN_HEADS) * n_q
    slab = n_exp // n_attn_steps
    share = max(1, n_attn_steps * LANES // D_MODEL)
    wrows = D_MODEL // (n_attn_steps // share)
    wstep = lambda h, i: (h * n_q + i) // share
    ob, u_bf, vt_bf, wqt, wo = pl.pallas_call(
        functools.partial(_attn_kernel, share=share),
        grid=(MLA_HEADS // ATTN_HEADS, n_q),
        in_specs=[pl.BlockSpec((ATTN_HEADS, TQ_ATTN, QK_PAD), lambda h, i: (h, i, 0)),
                  pl.BlockSpec((ATTN_HEADS, seq, QK_PAD), lambda h, i: (h, 0, 0)),
                  pl.BlockSpec((ATTN_HEADS, seq, LANES), lambda h, i: (h, 0, 0)),
                  pl.BlockSpec((slab, D_MODEL), lambda h, i: (h * n_q + i, 0)),
                  pl.BlockSpec((slab, D_MODEL), lambda h, i: (h * n_q + i, 0)),
                  pl.BlockSpec((D_MODEL, wrows), lambda h, i: (0, wstep(h, i))),
                  pl.BlockSpec((wrows, D_MODEL), lambda h, i: (wstep(h, i), 0))],
        out_specs=[pl.BlockSpec((TQ_ATTN, ATTN_HEADS * LANES), lambda h, i: (i, h)),
                   pl.BlockSpec((slab, D_MODEL), lambda h, i: (h * n_q + i, 0)),
                   pl.BlockSpec((D_MODEL, slab), lambda h, i: (0, h * n_q + i)),
                   pl.BlockSpec((wrows, D_MODEL), lambda h, i: (wstep(h, i), 0)),
                   pl.BlockSpec((wrows, D_MODEL), lambda h, i: (wstep(h, i), 0))],
        out_shape=[jax.ShapeDtypeStruct((seq, MLA_HEADS * LANES), BF16),
                   jax.ShapeDtypeStruct((n_exp, D_MODEL), BF16),
                   jax.ShapeDtypeStruct((D_MODEL, n_exp), BF16),
                   jax.ShapeDtypeStruct((D_MODEL, D_MODEL), BF16),
                   jax.ShapeDtypeStruct((D_MODEL, D_MODEL), BF16)],
        compiler_params=_params(("parallel", "arbitrary")),
        name="attn",
    )(q, k, v, peer_u[0], peer_v[0], peer_w_q[0], w_out[0])

    sk = peer_subkeys[0].reshape(PEER_HEADS * 2, PEER_N_KEYS, -1).astype(BF16)
    n_mid = seq // TM_MID
    mrows = lambda w: pl.BlockSpec((TM_MID, w), lambda i: (i, 0))
    tok = pl.BlockSpec((PEER_HEADS, PEER_N_KEYS, TM_MID), lambda i: (0, 0, i))
    tok_shape = lambda dt: jax.ShapeDtypeStruct((PEER_HEADS, PEER_N_KEYS, seq), dt)
    x1, xnt, n1, a1, r2, b2 = pl.pallas_call(
        _mid_kernel,
        grid=(n_mid,),
        in_specs=[mrows(D_MODEL), mrows(GMLP_WIDTH), mrows(MLA_HEADS * LANES),
                  pl.BlockSpec((GMLP_WIDTH, D_MODEL), lambda i: (0, 0)),
                  pl.BlockSpec((MLA_HEADS * LANES, D_MODEL), lambda i: (1, 0)),
                  _const_spec((1, D_MODEL)), _const_spec(wqt.shape), _const_spec(sk.shape)],
        out_specs=[mrows(D_MODEL), pl.BlockSpec((D_MODEL, TM_MID), lambda i: (0, i)), tok, tok, tok, tok],
        out_shape=[jax.ShapeDtypeStruct((seq, D_MODEL), F32), jax.ShapeDtypeStruct((D_MODEL, seq), BF16),
                   tok_shape(F32), tok_shape(F32), tok_shape(BF16), tok_shape(BF16)],
        scratch_shapes=[pltpu.VMEM((2 * PEER_HEADS, PEER_N_KEYS, TM_MID), F32),
                        pltpu.VMEM((2 * PEER_HEADS, PEER_TOPK, TM_MID), F32),
                        pltpu.VMEM((2 * PEER_HEADS, PEER_N_KEYS, TM_MID), F32)],
        compiler_params=_params(("parallel",)),
        name="mid",
    )(x2, oa, ob, wo, wo, row(norm_ffn_g[0]), wqt, sk)

    keys_per_step = EC_PEER // PEER_N_KEYS
    first_half = pl.BlockSpec((PEER_HEADS, keys_per_step, TT_PEER), lambda t, e: (0, e, t))
    second_half = pl.BlockSpec((PEER_HEADS, PEER_N_KEYS, TT_PEER), lambda t, e: (0, 0, t))
    y = pl.pallas_call(
        _peer_kernel,
        grid=(seq // TT_PEER, n_exp // EC_PEER),
        in_specs=[first_half, first_half, second_half, second_half,
                  pl.BlockSpec((D_MODEL, TT_PEER), lambda t, e: (0, t)),
                  pl.BlockSpec((EC_PEER, D_MODEL), lambda t, e: (e, 0)),
                  pl.BlockSpec((D_MODEL, EC_PEER), lambda t, e: (0, e)),
                  pl.BlockSpec((TT_PEER, D_MODEL), lambda t, e: (t, 0))],
        out_specs=pl.BlockSpec((TT_PEER, D_MODEL), lambda t, e: (t, 0)),
        out_shape=jax.ShapeDtypeStruct((seq, D_MODEL), F32),
        scratch_shapes=[pltpu.VMEM((D_MODEL, TT_PEER), F32), pltpu.VMEM((2, PEER_SUB, TT_PEER), BF16)],
        compiler_params=_params(("parallel", "arbitrary")),
        name="peer",
    )(n1, a1, r2, b2, xnt, u_bf, vt_bf, x1)
    return y[None]
```

```python
import functools

import jax
import jax.numpy as jnp
import numpy as np
from jax import lax
from jax.experimental import pallas as pl
from jax.experimental.pallas import tpu as pltpu

F32 = jnp.float32
BF16 = jnp.bfloat16

D_MODEL = 2048
CHUNK = 64
GMLP_HEADS = 8
GMLP_WIDTH = 1024
GMLP_BLOCK = 128
MLA_HEADS = 8
MLA_NOPE = 128
MLA_ROPE = 64
MLA_QK = MLA_NOPE + MLA_ROPE
MLA_Q_RANK = 512
MLA_KV_RANK = 256
ROPE_THETA = 10000.0
PEER_HEADS = 8
PEER_N_KEYS = 128
PEER_TOPK = 16
RMS_EPS = 1e-6
LANES = 128
QK_PAD = 2 * LANES

TM_FRONT = 256
TQ_ATTN = 512
TK_ATTN = 1024
ATTN_HEADS = 2
TM_MID = 256
TT_PEER = 512
EC_PEER = 1024
PEER_SUB = 256
VMEM_LIMIT = 56 * 1024 * 1024


def _rms_scale(v, width):
    return lax.rsqrt(jnp.sum(v * v, axis=-1, keepdims=True) * (1.0 / width) + RMS_EPS)


def _front_kernel(x_ref, gmix_ref, win_ref, wrope_ref, gv_ref, ws_ref, bsb_ref, gq_ref, wuq_ref, gkv_ref, wukv_ref,
                  gqn_ref, gqr_ref, gqrr_ref, gkn_ref, gkr_ref, gkrr_ref,
                  cos_t_ref, sin_t_ref, sins_t_ref, cos_o_ref, sin_o_ref, sins_o_ref,
                  oa_ref, q_ref, k_ref, v_ref):
    tm = x_ref.shape[0]
    x = x_ref[...]
    h = (x * _rms_scale(x, D_MODEL) * gmix_ref[...]).astype(BF16)
    c_rope = 2 * GMLP_WIDTH + MLA_Q_RANK + MLA_KV_RANK
    proj = jnp.dot(h, win_ref[:, :c_rope], preferred_element_type=F32)
    rope = jnp.dot(h, wrope_ref[...], preferred_element_type=F32)

    zu = jax.nn.gelu(proj[:, :GMLP_WIDTH])
    zv = jax.nn.gelu(proj[:, GMLP_WIDTH:2 * GMLP_WIDTH])
    vn = (zv * _rms_scale(zv, GMLP_WIDTH) * gv_ref[...]).astype(BF16)
    row = lax.broadcasted_iota(jnp.int32, (GMLP_BLOCK, GMLP_BLOCK), 0)
    col = lax.broadcasted_iota(jnp.int32, (GMLP_BLOCK, GMLP_BLOCK), 1)
    tri = col <= row
    for hh in range(GMLP_HEADS):
        w = jnp.where(tri, ws_ref[hh], 0.0).astype(BF16)
        cs = slice(hh * LANES, (hh + 1) * LANES)
        for r in range(tm // GMLP_BLOCK):
            rs = slice(r * GMLP_BLOCK, (r + 1) * GMLP_BLOCK)
            sv = jnp.dot(w, vn[rs, cs], preferred_element_type=F32) + bsb_ref[:, cs]
            oa_ref[rs, cs] = (zu[rs, cs] * sv).astype(BF16)

    c0 = 2 * GMLP_WIDTH
    cq = proj[:, c0:c0 + MLA_Q_RANK]
    ckv = proj[:, c0 + MLA_Q_RANK:c0 + MLA_Q_RANK + MLA_KV_RANK]
    kr = rope[:, :LANES]
    krr = rope[:, LANES:]
    cqn = (cq * _rms_scale(cq, MLA_Q_RANK) * gq_ref[...]).astype(BF16)
    qall = jnp.dot(cqn, wuq_ref[...], preferred_element_type=F32)
    ckvn = (ckv * _rms_scale(ckv, MLA_KV_RANK) * gkv_ref[...]).astype(BF16)
    kvall = jnp.dot(ckvn, wukv_ref[...], preferred_element_type=F32)

    cos_f = cos_t_ref[...] * cos_o_ref[...] - sin_t_ref[...] * sin_o_ref[...]
    sin_f = sins_t_ref[...] * cos_o_ref[...] + cos_t_ref[...] * sins_o_ref[...]
    lane = lax.broadcasted_iota(jnp.int32, (tm, LANES), 1)
    lo = lane < MLA_ROPE
    k_rot = kr * gkr_ref[...] * cos_f + krr * gkrr_ref[...] * sin_f
    ssq_kr = jnp.sum(jnp.where(lo, kr * kr, 0.0), axis=-1, keepdims=True)
    q_scale = MLA_QK ** -0.5
    nh = MLA_HEADS * MLA_NOPE
    for g in range(MLA_HEADS // 2):
        qr = qall[:, nh + g * LANES:nh + (g + 1) * LANES]
        qrr = qall[:, nh + 512 + g * LANES:nh + 512 + (g + 1) * LANES]
        q_rot = qr * gqr_ref[...] * cos_f + qrr * gqrr_ref[...] * sin_f
        qr2 = qr * qr
        ssq_half = (jnp.sum(jnp.where(lo, qr2, 0.0), axis=-1, keepdims=True),
                    jnp.sum(jnp.where(lo, 0.0, qr2), axis=-1, keepdims=True))
        for p in range(2):
            hh = 2 * g + p
            hs = slice(hh * LANES, (hh + 1) * LANES)
            qn = qall[:, hs]
            rinv = lax.rsqrt((jnp.sum(qn * qn, axis=-1, keepdims=True) + ssq_half[p]) * (1.0 / MLA_QK)
                             + RMS_EPS) * q_scale
            mine = lo if p == 0 else jnp.logical_not(lo)
            q_ref[hh, :, :LANES] = (qn * gqn_ref[...] * rinv).astype(BF16)
            q_ref[hh, :, LANES:] = (jnp.where(mine, q_rot, 0.0) * rinv).astype(BF16)
            kn = kvall[:, hs]
            rinvk = lax.rsqrt((jnp.sum(kn * kn, axis=-1, keepdims=True) + ssq_kr) * (1.0 / MLA_QK) + RMS_EPS)
            k_ref[hh, :, :LANES] = (kn * gkn_ref[...] * rinvk).astype(BF16)
            k_ref[hh, :, LANES:] = (k_rot * rinvk).astype(BF16)
            v_ref[hh] = kvall[:, nh + hh * LANES:nh + (hh + 1) * LANES].astype(BF16)


def _attn_kernel(q_ref, k_ref, v_ref, u_ref, pv_ref, wq_ref, wo_ref, o_ref, ub_ref, vt_ref, wqt_ref, wob_ref, *,
                 share):
    tq = q_ref.shape[1]
    qi = pl.program_id(1)
    ub_ref[...] = u_ref[...].astype(BF16)
    vt_ref[...] = pv_ref[...].astype(BF16)

    @pl.when((pl.program_id(0) * pl.num_programs(1) + qi) % share == 0)
    def _():
        wqt_ref[...] = wq_ref[...].T.astype(BF16)
        wob_ref[...] = wo_ref[...].astype(BF16)
    qs = [q_ref[j] for j in range(ATTN_HEADS)]

    def tile(kt, carry, masked):
        start = pl.multiple_of(kt * TK_ATTN, TK_ATTN)
        if masked:
            q_chunk = (qi * tq + lax.broadcasted_iota(jnp.int32, (tq, 1), 0)) // CHUNK
            k_chunk = (start + lax.broadcasted_iota(jnp.int32, (1, TK_ATTN), 1)) // CHUNK
            allowed = k_chunk <= q_chunk
        out = []
        for j in range(ATTN_HEADS):
            m, l, acc = carry[j]
            kk = k_ref[j, pl.ds(start, TK_ATTN), :]
            vv = v_ref[j, pl.ds(start, TK_ATTN), :]
            s = lax.dot_general(qs[j], kk, (((1,), (1,)), ((), ())), preferred_element_type=F32)
            if masked:
                s = jnp.where(allowed, s, -jnp.inf)
            m_new = jnp.maximum(m, jnp.max(s, axis=-1, keepdims=True))
            alpha = jnp.exp(m - m_new)
            p = jnp.exp(s - m_new)
            l = alpha * l + jnp.sum(p, axis=-1, keepdims=True)
            acc = alpha * acc + jnp.dot(p.astype(BF16), vv, preferred_element_type=F32)
            out.append((m_new, l, acc))
        return tuple(out)

    init = tuple((jnp.full((tq, 1), -jnp.inf, F32), jnp.zeros((tq, 1), F32), jnp.zeros((tq, LANES), F32))
                 for _ in range(ATTN_HEADS))
    n_full = (qi * tq) // TK_ATTN
    carry = lax.fori_loop(0, n_full, lambda kt, c: tile(kt, c, False), init)
    carry = tile(n_full, carry, True)
    for j in range(ATTN_HEADS):
        _, l, acc = carry[j]
        o_ref[:, j * LANES:(j + 1) * LANES] = (acc / l).astype(BF16)


def _topk_ranks(s):
    n = s.shape[0]
    idx = lax.broadcasted_iota(jnp.int32, s.shape, 0)
    rank = jnp.full(s.shape, float(PEER_TOPK), F32)
    vals = []
    for kk in range(PEER_TOPK):
        m = jnp.max(s, axis=0, keepdims=True)
        first = jnp.min(jnp.where(s == m, idx, n), axis=0, keepdims=True)
        sel = idx == first
        rank = jnp.where(sel, float(kk), rank)
        s = jnp.where(sel, -jnp.inf, s)
        vals.append(m)
    return jnp.concatenate(vals, axis=0), rank


def _topk_fast(s):
    rank = jnp.full(s.shape, float(PEER_TOPK), F32)
    vals = []
    for kk in range(PEER_TOPK):
        m = jnp.max(s, axis=0, keepdims=True)
        hit = s == m
        rank = jnp.where(hit, float(kk), rank)
        s = jnp.where(hit, -jnp.inf, s)
        vals.append(m)
    taken = jnp.sum(jnp.where(rank < PEER_TOPK, 1.0, 0.0), axis=0, keepdims=True)
    return jnp.concatenate(vals, axis=0), rank, taken


def _pair_counts(v1, v2):
    t = v1.shape[1]
    row8 = lax.broadcasted_iota(jnp.int32, (8, t), 0)
    cands = [v1[0:1] + v2[0:8], v1[0:1] + v2[8:16]]
    for a in range(1, 8):
        cands.append(jnp.where(row8 < PEER_TOPK // (a + 1), v1[a:a + 1] + v2[0:8], -jnp.inf))
    cands.append(v1[8:16] + v2[0:1])
    work = list(cands)
    cum = jnp.zeros((1, t), F32)
    tau = jnp.zeros((1, t), F32)
    above = jnp.zeros((1, t), F32)
    for _ in range(PEER_TOPK):
        m = jnp.max(functools.reduce(jnp.maximum, work), axis=0, keepdims=True)
        hit = [c == m for c in work]
        cnt = jnp.sum(functools.reduce(jnp.add, [jnp.where(hh, 1.0, 0.0) for hh in hit]), axis=0, keepdims=True)
        active = cum < PEER_TOPK
        tau = jnp.where(active, m, tau)
        above = jnp.where(active, cum, above)
        cum = cum + cnt
        work = [jnp.where(hh, -jnp.inf, c) for hh, c in zip(hit, work)]
    need = PEER_TOPK - above
    rows = [[cands[0], cands[1]]] + [[cands[a + 1]] for a in range(1, 8)]
    n_rows = []
    for a in range(PEER_TOPK):
        cs = rows[a] if a < 8 else [cands[9][a - 8:a - 7]]
        gt = functools.reduce(jnp.add, [jnp.sum(jnp.where(c > tau, 1.0, 0.0), axis=0, keepdims=True) for c in cs])
        eq = functools.reduce(jnp.add, [jnp.sum(jnp.where(c == tau, 1.0, 0.0), axis=0, keepdims=True) for c in cs])
        take = jnp.minimum(eq, jnp.maximum(need, 0.0))
        need = need - eq
        n_rows.append(gt + take)
    return jnp.concatenate(n_rows, axis=0)


def _mid_kernel(x_ref, oa_ref, ob_ref, woa_ref, wob_ref, gffn_ref, wqt_ref, sk_ref,
                x1_ref, xnt_ref, n1_ref, a_ref, r2_ref, b_ref, s_ref, vals_ref, rank_ref):
    tm = x_ref.shape[0]
    x1 = (x_ref[...] + jnp.dot(oa_ref[...], woa_ref[...], preferred_element_type=F32)
          + jnp.dot(ob_ref[...], wob_ref[...], preferred_element_type=F32))
    x1_ref[...] = x1
    xn = x1 * _rms_scale(x1, D_MODEL) * gffn_ref[...]
    xnt = xn.T.astype(BF16)
    xnt_ref[...] = xnt
    qt = jnp.dot(wqt_ref[...], xnt, preferred_element_type=F32).astype(BF16)

    n_hp = 2 * PEER_HEADS
    excess = jnp.zeros((1, tm), F32)
    for hp in range(n_hp):
        sp = jnp.dot(sk_ref[hp], qt[hp * LANES:(hp + 1) * LANES, :], preferred_element_type=F32)
        s_ref[hp] = sp
        vp, rp, taken = _topk_fast(sp)
        vals_ref[hp] = vp
        rank_ref[hp] = rp
        excess = jnp.maximum(excess, taken - PEER_TOPK)

    @pl.when(jnp.max(excess) > 0.0)
    def _():
        for hp in range(n_hp):
            vp, rp = _topk_ranks(s_ref[hp])
            vals_ref[hp] = vp
            rank_ref[hp] = rp

    rowk = lax.broadcasted_iota(jnp.int32, (PEER_TOPK, tm), 0).astype(F32)
    for hh in range(PEER_HEADS):
        v1, v2 = vals_ref[2 * hh], vals_ref[2 * hh + 1]
        n = _pair_counts(v1, v2)
        e1 = jnp.exp(v1 - v1[0:1])
        e2 = jnp.exp(v2 - v2[0:1])
        paired = jnp.zeros((PEER_TOPK, tm), F32)
        for a in range(PEER_TOPK):
            paired = paired + jnp.where(rowk < n[a:a + 1], e1[a:a + 1], 0.0)
        z = jnp.sum(paired * e2, axis=0, keepdims=True)
        rank1 = rank_ref[2 * hh]
        n1 = jnp.zeros((PEER_N_KEYS, tm), F32)
        for a in range(PEER_TOPK):
            n1 = jnp.where(rank1 == float(a), n[a:a + 1], n1)
        n1_ref[hh] = n1
        a_ref[hh] = jnp.exp(s_ref[2 * hh] - v1[0:1]) / z
        r2_ref[hh] = rank_ref[2 * hh + 1].astype(BF16)
        b_ref[hh] = jnp.exp(s_ref[2 * hh + 1] - v2[0:1]).astype(BF16)


def _peer_kernel(n1_ref, a_ref, r2_ref, b_ref, xnt_ref, u_ref, vt_ref, x1_ref, y_ref, g_ref):
    e = pl.program_id(1)
    tt = xnt_ref.shape[1]
    keys = PEER_SUB // PEER_N_KEYS
    pack = 16
    reps = (PEER_N_KEYS // pack, 1)

    @pl.when(e == 0)
    def _():
        y_ref[...] = x1_ref[...]

    def weights(k):
        w = None
        for hh in range(PEER_HEADS):
            n16 = jnp.broadcast_to(n1_ref[hh, k:k + 1, :], (pack, tt)).astype(BF16)
            a16 = jnp.broadcast_to(a_ref[hh, k:k + 1, :], (pack, tt)).astype(BF16)
            term = jnp.where(r2_ref[hh] < jnp.tile(n16, reps), b_ref[hh], jnp.zeros((), BF16)) * jnp.tile(a16, reps)
            w = term if w is None else w + term
        return w

    def activations(c):
        rows = slice(c * PEER_SUB, (c + 1) * PEER_SUB)
        at = jnp.dot(u_ref[rows, :], xnt_ref[...], preferred_element_type=F32)
        g_ref[c % 2] = jax.nn.gelu(at).astype(BF16)

    def accumulate(c):
        rows = slice(c * PEER_SUB, (c + 1) * PEER_SUB)
        wa = jnp.concatenate([g_ref[c % 2, k * PEER_N_KEYS:(k + 1) * PEER_N_KEYS, :] * weights(c * keys + k)
                              for k in range(keys)], axis=0)
        y_ref[...] += lax.dot_general(wa, vt_ref[rows, :], (((0,), (0,)), ((), ())), preferred_element_type=F32)

    n_sub = EC_PEER // PEER_SUB
    activations(0)
    for c in range(n_sub):
        if c + 1 < n_sub:
            activations(c + 1)
        accumulate(c)


def _const_spec(shape):
    zeros = (0,) * len(shape)
    return pl.BlockSpec(shape, lambda *_: zeros)


def _params(semantics):
    return pltpu.CompilerParams(dimension_semantics=semantics, vmem_limit_bytes=VMEM_LIMIT)


def kernel(x, norm_mix_g, w_in, gmlp_vnorm_g, gmlp_ws, gmlp_bs, mla_qnorm_g, mla_w_uq, mla_kvnorm_g, mla_w_ukv,
           mla_qk_q_g, mla_qk_k_g, w_out, norm_ffn_g, peer_w_q, peer_subkeys, peer_u, peer_v):
    bsz, seq, d = x.shape
    assert bsz == 1 and d == D_MODEL and seq % TK_ATTN == 0 and TK_ATTN % TQ_ATTN == 0 and seq % TT_PEER == 0
    assert peer_u.shape[1] % ((MLA_HEADS // ATTN_HEADS) * (seq // TQ_ATTN) * LANES) == 0
    x2 = x[0]
    n_exp = peer_u.shape[1]

    rot = (np.arange(MLA_ROPE) + MLA_ROPE // 2) % MLA_ROPE
    c_rope = 2 * GMLP_WIDTH + MLA_Q_RANK + MLA_KV_RANK
    w_in_bf = w_in[0].astype(BF16)
    w_rope = w_in_bf[:, c_rope:]
    w_rope_ext = jnp.concatenate([w_rope, w_rope, w_rope[:, rot], w_rope[:, rot]], axis=1)
    wq3 = mla_w_uq[0].reshape(MLA_Q_RANK, MLA_HEADS, MLA_QK)
    wq_rope = wq3[:, :, MLA_NOPE:]
    w_uq_ext = jnp.concatenate([wq3[:, :, :MLA_NOPE].reshape(MLA_Q_RANK, -1),
                                wq_rope.reshape(MLA_Q_RANK, -1),
                                wq_rope[:, :, rot].reshape(MLA_Q_RANK, -1)], axis=1).astype(BF16)
    wkv3 = mla_w_ukv[0].reshape(MLA_KV_RANK, MLA_HEADS, MLA_NOPE + LANES)
    w_ukv_ext = jnp.concatenate([wkv3[:, :, :MLA_NOPE].reshape(MLA_KV_RANK, -1),
                                 wkv3[:, :, MLA_NOPE:].reshape(MLA_KV_RANK, -1)], axis=1).astype(BF16)
    row = lambda v: v.reshape(1, -1)
    gq, gk = mla_qk_q_g[0], mla_qk_k_g[0]
    dup = lambda v: row(jnp.concatenate([v, v]))
    bsb = jnp.repeat(gmlp_bs[0].T, LANES, axis=1)
    inv_freq = ROPE_THETA ** (-jnp.arange(0, MLA_ROPE, 2, dtype=F32) / MLA_ROPE)
    n_front = seq // TM_FRONT
    lanes4 = lambda t: jnp.concatenate([t, t, t, t], axis=1)
    signed = lambda t: jnp.concatenate([-t, t, -t, t], axis=1)
    ang_t = (jnp.arange(n_front, dtype=F32) * TM_FRONT)[:, None] * inv_freq[None, :]
    ang_o = jnp.arange(TM_FRONT, dtype=F32)[:, None] * inv_freq[None, :]
    tile3 = lambda t: t.reshape(n_front, 1, LANES)
    cos_t, sin_t, sins_t = tile3(lanes4(jnp.cos(ang_t))), tile3(lanes4(jnp.sin(ang_t))), tile3(signed(jnp.sin(ang_t)))
    cos_o, sin_o, sins_o = lanes4(jnp.cos(ang_o)), lanes4(jnp.sin(ang_o)), signed(jnp.sin(ang_o))

    rows = lambda w: pl.BlockSpec((TM_FRONT, w), lambda i: (i, 0))
    per_tile = pl.BlockSpec((None, 1, LANES), lambda i: (i, 0, 0))
    heads = lambda w: pl.BlockSpec((MLA_HEADS, TM_FRONT, w), lambda i: (0, i, 0))
    oa, q, k, v = pl.pallas_call(
        _front_kernel,
        grid=(n_front,),
        in_specs=[rows(D_MODEL), _const_spec((1, D_MODEL)), _const_spec(w_in_bf.shape), _const_spec(w_rope_ext.shape),
                  _const_spec((1, GMLP_WIDTH)), _const_spec(gmlp_ws[0].shape), _const_spec(bsb.shape),
                  _const_spec((1, MLA_Q_RANK)), _const_spec(w_uq_ext.shape),
                  _const_spec((1, MLA_KV_RANK)), _const_spec(w_ukv_ext.shape),
                  _const_spec((1, LANES)), _const_spec((1, LANES)), _const_spec((1, LANES)),
                  _const_spec((1, LANES)), _const_spec((1, LANES)), _const_spec((1, LANES)),
                  per_tile, per_tile, per_tile,
                  _const_spec((TM_FRONT, LANES)), _const_spec((TM_FRONT, LANES)), _const_spec((TM_FRONT, LANES))],
        out_specs=[rows(GMLP_WIDTH), heads(QK_PAD), heads(QK_PAD), heads(LANES)],
        out_shape=[jax.ShapeDtypeStruct((seq, GMLP_WIDTH), BF16),
                   jax.ShapeDtypeStruct((MLA_HEADS, seq, QK_PAD), BF16),
                   jax.ShapeDtypeStruct((MLA_HEADS, seq, QK_PAD), BF16),
                   jax.ShapeDtypeStruct((MLA_HEADS, seq, LANES), BF16)],
        compiler_params=_params(("parallel",)),
        name="front",
    )(x2, row(norm_mix_g[0]), w_in_bf, w_rope_ext, row(gmlp_vnorm_g[0]), gmlp_ws[0], bsb,
      row(mla_qnorm_g[0]), w_uq_ext, row(mla_kvnorm_g[0]), w_ukv_ext,
      row(gq[:MLA_NOPE]), dup(gq[MLA_NOPE:]), dup(gq[MLA_NOPE:][rot]),
      row(gk[:MLA_NOPE]), dup(gk[MLA_NOPE:]), dup(gk[MLA_NOPE:][rot]),
      cos_t, sin_t, sins_t, cos_o, sin_o, sins_o)

    n_q = seq // TQ_ATTN
    n_attn_steps = (MLA_HEADS // ATTN_HEADS) * n_q
    slab = n_exp // n_attn_steps
    share = max(1, n_attn_steps * LANES // D_MODEL)
    wrows = D_MODEL // (n_attn_steps // share)
    wstep = lambda h, i: (h * n_q + i) // share
    ob, u_bf, vt_bf, wqt, wo = pl.pallas_call(
        functools.partial(_attn_kernel, share=share),
        grid=(MLA_HEADS // ATTN_HEADS, n_q),
        in_specs=[pl.BlockSpec((ATTN_HEADS, TQ_ATTN, QK_PAD), lambda h, i: (h, i, 0)),
                  pl.BlockSpec((ATTN_HEADS, seq, QK_PAD), lambda h, i: (h, 0, 0)),
                  pl.BlockSpec((ATTN_HEADS, seq, LANES), lambda h, i: (h, 0, 0)),
                  pl.BlockSpec((slab, D_MODEL), lambda h, i: (h * n_q + i, 0)),
                  pl.BlockSpec((slab, D_MODEL), lambda h, i: (h * n_q + i, 0)),
                  pl.BlockSpec((D_MODEL, wrows), lambda h, i: (0, wstep(h, i))),
                  pl.BlockSpec((wrows, D_MODEL), lambda h, i: (wstep(h, i), 0))],
        out_specs=[pl.BlockSpec((TQ_ATTN, ATTN_HEADS * LANES), lambda h, i: (i, h)),
                   pl.BlockSpec((slab, D_MODEL), lambda h, i: (h * n_q + i, 0)),
                   pl.BlockSpec((slab, D_MODEL), lambda h, i: (h * n_q + i, 0)),
                   pl.BlockSpec((wrows, D_MODEL), lambda h, i: (wstep(h, i), 0)),
                   pl.BlockSpec((wrows, D_MODEL), lambda h, i: (wstep(h, i), 0))],
        out_shape=[jax.ShapeDtypeStruct((seq, MLA_HEADS * LANES), BF16),
                   jax.ShapeDtypeStruct((n_exp, D_MODEL), BF16),
                   jax.ShapeDtypeStruct((n_exp, D_MODEL), BF16),
                   jax.ShapeDtypeStruct((D_MODEL, D_MODEL), BF16),
                   jax.ShapeDtypeStruct((D_MODEL, D_MODEL), BF16)],
        compiler_params=_params(("parallel", "arbitrary")),
        name="attn",
    )(q, k, v, peer_u[0], peer_v[0], peer_w_q[0], w_out[0])

    sk = peer_subkeys[0].reshape(PEER_HEADS * 2, PEER_N_KEYS, -1).astype(BF16)
    n_mid = seq // TM_MID
    mrows = lambda w: pl.BlockSpec((TM_MID, w), lambda i: (i, 0))
    tok = pl.BlockSpec((PEER_HEADS, PEER_N_KEYS, TM_MID), lambda i: (0, 0, i))
    tok_shape = lambda dt: jax.ShapeDtypeStruct((PEER_HEADS, PEER_N_KEYS, seq), dt)
    x1, xnt, n1, a1, r2, b2 = pl.pallas_call(
        _mid_kernel,
        grid=(n_mid,),
        in_specs=[mrows(D_MODEL), mrows(GMLP_WIDTH), mrows(MLA_HEADS * LANES),
                  pl.BlockSpec((GMLP_WIDTH, D_MODEL), lambda i: (0, 0)),
                  pl.BlockSpec((MLA_HEADS * LANES, D_MODEL), lambda i: (1, 0)),
                  _const_spec((1, D_MODEL)), _const_spec(wqt.shape), _const_spec(sk.shape)],
        out_specs=[mrows(D_MODEL), pl.BlockSpec((D_MODEL, TM_MID), lambda i: (0, i)), tok, tok, tok, tok],
        out_shape=[jax.ShapeDtypeStruct((seq, D_MODEL), F32), jax.ShapeDtypeStruct((D_MODEL, seq), BF16),
                   tok_shape(F32), tok_shape(F32), tok_shape(BF16), tok_shape(BF16)],
        scratch_shapes=[pltpu.VMEM((2 * PEER_HEADS, PEER_N_KEYS, TM_MID), F32),
                        pltpu.VMEM((2 * PEER_HEADS, PEER_TOPK, TM_MID), F32),
                        pltpu.VMEM((2 * PEER_HEADS, PEER_N_KEYS, TM_MID), F32)],
        compiler_params=_params(("parallel",)),
        name="mid",
    )(x2, oa, ob, wo, wo, row(norm_ffn_g[0]), wqt, sk)

    keys_per_step = EC_PEER // PEER_N_KEYS
    first_half = pl.BlockSpec((PEER_HEADS, keys_per_step, TT_PEER), lambda t, e: (0, e, t))
    second_half = pl.BlockSpec((PEER_HEADS, PEER_N_KEYS, TT_PEER), lambda t, e: (0, 0, t))
    y = pl.pallas_call(
        _peer_kernel,
        grid=(seq // TT_PEER, n_exp // EC_PEER),
        in_specs=[first_half, first_half, second_half, second_half,
                  pl.BlockSpec((D_MODEL, TT_PEER), lambda t, e: (0, t)),
                  pl.BlockSpec((EC_PEER, D_MODEL), lambda t, e: (e, 0)),
                  pl.BlockSpec((EC_PEER, D_MODEL), lambda t, e: (e, 0)),
                  pl.BlockSpec((TT_PEER, D_MODEL), lambda t, e: (t, 0))],
        out_specs=pl.BlockSpec((TT_PEER, D_MODEL), lambda t, e: (t, 0)),
        out_shape=jax.ShapeDtypeStruct((seq, D_MODEL), F32),
        scratch_shapes=[pltpu.VMEM((2, PEER_SUB, TT_PEER), BF16)],
        compiler_params=_params(("parallel", "arbitrary")),
        name="peer",
    )(n1, a1, r2, b2, xnt, u_bf, vt_bf, x1)
    return y[None]
```
